```python
import math
import jax
import jax.numpy as jnp
from jax import lax
import numpy as np

D_MODEL = 1024
BATCH = 4
SEQ = 8192
DEPTH = 1

GLA_HEADS = 4
GLA_DK = 128
GLA_DV = 256
GLA_GATE_RANK = 16
GLA_GATE_NORMALIZER = 16.0
GLA_CHUNK = 64

SWA_Q_HEADS = 16
SWA_KV_HEADS = 4
SWA_HEAD_DIM = 64
SWA_WINDOW = 128

REL_BUCKETS = 32
REL_MAX_DISTANCE = 128

PEER_HEADS = 8
PEER_N_KEYS = 128
PEER_N_EXPERTS = PEER_N_KEYS * PEER_N_KEYS
PEER_QUERY_DIM = 256
PEER_TOPK = 16
PEER_TOKEN_BLOCK = 128

EPS = 1e-6

IN_SPLITS = (
    GLA_HEADS * GLA_DK,
    GLA_HEADS * GLA_DK,
    GLA_HEADS * GLA_DV,
    GLA_HEADS * GLA_DV,
    GLA_GATE_RANK,
    SWA_Q_HEADS * SWA_HEAD_DIM,
    SWA_KV_HEADS * SWA_HEAD_DIM,
    SWA_KV_HEADS * SWA_HEAD_DIM,
    D_MODEL,
    D_MODEL,
)
IN_WIDTH = sum(IN_SPLITS)

kernel_name = 'hybrid_gla_swa_peer_block'


def rms_norm(x, gain):
    xf = x.astype(jnp.float32)
    y = xf * lax.rsqrt(jnp.mean(xf * xf, axis=-1, keepdims=True) + EPS)
    return (y * gain.astype(jnp.float32)).astype(x.dtype)


def split_columns(proj):
    parts, start = [], 0
    for width in IN_SPLITS:
        parts.append(proj[..., start:start + width])
        start += width
    return parts


def gla_branch(q, k, v, r, a_low, w_gate_up, b_gate_up, out_gain):
    B, S, _ = q.shape
    H, dk, dv, C = GLA_HEADS, GLA_DK, GLA_DV, GLA_CHUNK
    n = S // C
    f32 = jnp.float32
    log_alpha = jax.nn.log_sigmoid((a_low @ w_gate_up + b_gate_up).astype(f32)) / GLA_GATE_NORMALIZER

    def to_chunks(t, d):
        return t.astype(f32).reshape(B, n, C, H, d).transpose(1, 0, 3, 2, 4)

    qc = to_chunks(q, dk) * (dk ** -0.5)
    kc = to_chunks(k, dk)
    vc = to_chunks(v, dv)
    gc = to_chunks(log_alpha, dk)
    causal = jnp.tril(jnp.ones((C, C), dtype=bool))[:, :, None]

    def chunk_step(state, inp):
        qb, kb, vb, gb = inp
        b = jnp.cumsum(gb, axis=-2)
        diff = b[..., :, None, :] - b[..., None, :, :]
        decay = jnp.exp(jnp.where(causal, diff, -jnp.inf))
        scores = jnp.einsum('bhid,bhjd,bhijd->bhij', qb, kb, decay)
        out = (jnp.einsum('bhij,bhjv->bhiv', scores, vb)
               + jnp.einsum('bhid,bhdv->bhiv', qb * jnp.exp(b), state))
        b_last = b[..., -1:, :]
        new_state = (state * jnp.swapaxes(jnp.exp(b_last), -1, -2)
                     + jnp.einsum('bhjd,bhjv->bhdv', kb * jnp.exp(b_last - b), vb))
        return new_state, out

    state0 = jnp.zeros((B, H, dk, dv), f32)
    _, outs = lax.scan(chunk_step, state0, (qc, kc, vc, gc))
    o = outs.transpose(1, 0, 3, 2, 4).reshape(B, S, H, dv)
    o = rms_norm(o, out_gain)
    o = o * jax.nn.silu(r.astype(f32).reshape(B, S, H, dv))
    return o.reshape(B, S, H * dv).astype(q.dtype)


def t5_bucket(rel):
    max_exact = REL_BUCKETS // 2
    rel_f = jnp.maximum(rel, 1).astype(jnp.float32)
    large = max_exact + (jnp.log(rel_f / max_exact)
                         / math.log(REL_MAX_DISTANCE / max_exact)
                         * (REL_BUCKETS - max_exact)).astype(jnp.int32)
    large = jnp.minimum(large, REL_BUCKETS - 1)
    return jnp.where(rel < max_exact, rel, large)


def swa_branch(q, k, v, q_gain, k_gain, sinks, rel_bias):
    B, S, _ = q.shape
    Hq, Hkv, hd, W = SWA_Q_HEADS, SWA_KV_HEADS, SWA_HEAD_DIM, SWA_WINDOW
    G = Hq // Hkv
    nb = S // W
    f32 = jnp.float32
    qn = rms_norm(q.astype(f32).reshape(B, S, Hq, hd), q_gain) * (hd ** -0.5)
    kn = rms_norm(k.astype(f32).reshape(B, S, Hkv, hd), k_gain)
    vf = v.astype(f32).reshape(B, S, Hkv, hd)
    qb = qn.reshape(B, nb, W, Hkv, G, hd)

    def band(t):
        tb = t.reshape(B, nb, W, Hkv, hd)
        prev = jnp.pad(tb, ((0, 0), (1, 0), (0, 0), (0, 0), (0, 0)))[:, :-1]
        return jnp.concatenate([prev, tb], axis=2)

    kb, vb = band(kn), band(vf)
    qi = jnp.arange(W)[:, None]
    kj = jnp.arange(2 * W)[None, :]
    rel = qi + W - kj
    in_window = (rel >= 0) & (rel < W)
    blk = jnp.arange(nb)[:, None, None]
    valid = in_window[None] & ((blk > 0) | (kj[None] >= W))
    bias = rel_bias[t5_bucket(jnp.clip(rel, 0, None))]
    bias = bias.astype(f32).transpose(2, 0, 1).reshape(Hkv, G, W, 2 * W)

    logits = jnp.einsum('bnqhgd,bnkhd->bnhgqk', qb, kb) + bias
    logits = jnp.where(valid[None, :, None, None], logits, -jnp.inf)
    sink = sinks.astype(f32).reshape(Hkv, G, 1, 1)
    m = jnp.maximum(jnp.max(logits, axis=-1, keepdims=True), sink)
    e = jnp.exp(logits - m)
    probs = e / (jnp.sum(e, axis=-1, keepdims=True) + jnp.exp(sink - m))
    o = jnp.einsum('bnhgqk,bnkhd->bnqhgd', probs, vb)
    return o.reshape(B, S, Hq * hd).astype(q.dtype)


def peer_ffn(x, w_query, sub_keys, expert_down, expert_up):
    B, S, D = x.shape
    T = B * S
    H, K, nk = PEER_HEADS, PEER_TOPK, PEER_N_KEYS
    f32 = jnp.float32
    xt = x.reshape(T, D)
    q = (xt @ w_query).astype(f32).reshape(T, H, 2, PEER_QUERY_DIM // 2)
    scores = jnp.einsum('thpd,hpkd->thpk', q, sub_keys.astype(f32))
    top_s, top_i = lax.top_k(scores, K)
    cand_s = (top_s[:, :, 0, :, None] + top_s[:, :, 1, None, :]).reshape(T, H, K * K)
    cand_i = (top_i[:, :, 0, :, None] * nk + top_i[:, :, 1, None, :]).reshape(T, H, K * K)
    best_s, best_pos = lax.top_k(cand_s, K)
    expert_idx = jnp.take_along_axis(cand_i, best_pos, axis=-1)
    gate = jax.nn.softmax(best_s, axis=-1)

    n_blk = T // PEER_TOKEN_BLOCK

    def expert_block(args):
        xb, idx, g = args
        u = expert_down[idx]
        act = jax.nn.gelu(jnp.einsum('td,ted->te', xb, u).astype(f32), approximate=False)
        w = (g * act).astype(x.dtype)
        return jnp.einsum('te,ted->td', w, expert_up[idx])

    out = lax.map(expert_block, (xt.reshape(n_blk, PEER_TOKEN_BLOCK, D),
                                 expert_idx.reshape(n_blk, PEER_TOKEN_BLOCK, H * K),
                                 gate.reshape(n_blk, PEER_TOKEN_BLOCK, H * K)))
    return out.reshape(B, S, D)


def setup_inputs(seed: int = 0) -> dict:
    key = jax.random.key(seed)
    ks = jax.random.split(key, 20)
    L, D = DEPTH, D_MODEL
    f32 = jnp.float32

    def normal(k, shape, scale):
        return jax.random.normal(k, shape, f32) * scale

    def gain(k, shape):
        return 1.0 + 0.01 * jax.random.normal(k, shape, f32)

    return {
        'x': normal(ks[0], (BATCH, SEQ, D), 1.0),
        'norm_mix_gain': gain(ks[1], (L, D)),
        'w_in': normal(ks[2], (L, D, IN_WIDTH), D ** -0.5),
        'w_gate_up': normal(ks[3], (L, GLA_GATE_RANK, GLA_HEADS * GLA_DK), GLA_GATE_RANK ** -0.5),
        'b_gate_up': normal(ks[4], (L, GLA_HEADS * GLA_DK), 0.1),
        'gla_out_gain': gain(ks[5], (L, GLA_DV)),
        'swa_q_gain': gain(ks[6], (L, SWA_HEAD_DIM)),
        'swa_k_gain': gain(ks[7], (L, SWA_HEAD_DIM)),
        'swa_sinks': normal(ks[8], (L, SWA_Q_HEADS), 0.5),
        'rel_bias': normal(ks[9], (REL_BUCKETS, SWA_Q_HEADS), 0.5),
        'w_branch_gla': normal(ks[10], (L, GLA_HEADS * GLA_DV, D), (GLA_HEADS * GLA_DV) ** -0.5),
        'w_branch_swa': normal(ks[11], (L, SWA_Q_HEADS * SWA_HEAD_DIM, D), (SWA_Q_HEADS * SWA_HEAD_DIM) ** -0.5),
        'w_out': normal(ks[12], (L, D, D), D ** -0.5),
        'norm_ffn_gain': gain(ks[13], (L, D)),
        'peer_w_query': normal(ks[14], (L, D, PEER_HEADS * PEER_QUERY_DIM), D ** -0.5),
        'peer_sub_keys': normal(ks[15], (L, PEER_HEADS, 2, PEER_N_KEYS, PEER_QUERY_DIM // 2), (PEER_QUERY_DIM // 2) ** -0.5),
        'peer_expert_down': normal(ks[16], (L, PEER_N_EXPERTS, D), D ** -0.5),
        'peer_expert_up': normal(ks[17], (L, PEER_N_EXPERTS, D), (PEER_HEADS * PEER_TOPK) ** -0.5),
    }


def reference(x, norm_mix_gain, w_in, w_gate_up, b_gate_up, gla_out_gain,
              swa_q_gain, swa_k_gain, swa_sinks, rel_bias, w_branch_gla,
              w_branch_swa, w_out, norm_ffn_gain, peer_w_query, peer_sub_keys,
              peer_expert_down, peer_expert_up):
    h = x
    for layer in range(DEPTH):
        xn = rms_norm(h, norm_mix_gain[layer])
        proj = xn @ w_in[layer]
        gq, gk, gv, gr, ga, sq, sk, sv, gate_a, gate_b = split_columns(proj)
        y_gla = gla_branch(gq, gk, gv, gr, ga, w_gate_up[layer], b_gate_up[layer],
                           gla_out_gain[layer])
        y_swa = swa_branch(sq, sk, sv, swa_q_gain[layer], swa_k_gain[layer],
                           swa_sinks[layer], rel_bias)
        merged = (jax.nn.sigmoid(gate_a) * (y_gla @ w_branch_gla[layer])
                  + jax.nn.sigmoid(gate_b) * (y_swa @ w_branch_swa[layer]))
        h = h + merged @ w_out[layer]
        hn = rms_norm(h, norm_ffn_gain[layer])
        h = h + peer_ffn(hn, peer_w_query[layer], peer_sub_keys[layer],
                         peer_expert_down[layer], peer_expert_up[layer])
    return h
```

```python
import functools
import math

import numpy as np
import jax
import jax.numpy as jnp
from jax import lax
from jax.experimental import pallas as pl
from jax.experimental.pallas import tpu as pltpu

F32 = jnp.float32
BF16 = jnp.bfloat16

EPS = 1e-6

GLA_HEADS = 4
GLA_DK = 128
GLA_DV = 256
GLA_GATE_RANK = 16
GLA_GATE_NORMALIZER = 16.0
GLA_CHUNK = 64
GLA_SUB = 16

SWA_Q_HEADS = 16
SWA_KV_HEADS = 4
SWA_HEAD_DIM = 64
SWA_WINDOW = 128
REL_BUCKETS = 32
REL_MAX_DISTANCE = 128

PEER_HEADS = 8
PEER_N_KEYS = 128
PEER_HALF_DIM = 128
PEER_TOPK = 16

LANES = 128
VMEM_LIMIT = 56 * 1024 * 1024


def _cparams(semantics):
    return pltpu.CompilerParams(dimension_semantics=semantics,
                                vmem_limit_bytes=VMEM_LIMIT)


def _const_spec(shape):
    nd = len(shape)
    return pl.BlockSpec(shape, lambda *_: (0,) * nd)


def _in_proj_kernel(x_ref, g_ref, w_ref, *out_refs):
    x = x_ref[...]
    ms = jnp.mean(x * x, axis=-1, keepdims=True)
    xn = (x * lax.rsqrt(ms + EPS) * g_ref[...]).astype(BF16)
    off = 0
    for o_ref in out_refs:
        width = o_ref.shape[1]
        o_ref[...] = jnp.dot(xn, w_ref[:, off:off + width],
                             preferred_element_type=F32).astype(o_ref.dtype)
        off += width


def _in_proj(x2, gain, w_cat, widths, dtypes, tm):
    t, d = x2.shape
    nw = w_cat.shape[1]
    out_shape = [jax.ShapeDtypeStruct((t, w), dt) for w, dt in zip(widths, dtypes)]
    out_specs = [pl.BlockSpec((tm, w), lambda i: (i, 0)) for w in widths]
    return pl.pallas_call(
        _in_proj_kernel,
        grid=(t // tm,),
        in_specs=[pl.BlockSpec((tm, d), lambda i: (i, 0)),
                  _const_spec((1, d)),
                  _const_spec((d, nw))],
        out_specs=out_specs,
        out_shape=out_shape,
        compiler_params=_cparams(("parallel",)),
        name="in_proj",
    )(x2, gain, w_cat)


def _gla_kernel(q_ref, k_ref, v_ref, r_ref, ga_ref, wup_ref, bup_ref, gain_ref,
                o_ref, state_ref, *, n_chunks):
    c_rows, sub = GLA_CHUNK, GLA_SUB
    n_sub = c_rows // sub

    @pl.when(pl.program_id(2) == 0)
    def _():
        state_ref[...] = jnp.zeros_like(state_ref)

    row = lax.broadcasted_iota(jnp.int32, (c_rows, c_rows), 0)
    col = lax.broadcasted_iota(jnp.int32, (c_rows, c_rows), 1)
    tril = (row >= col).astype(F32)
    row_s = lax.broadcasted_iota(jnp.int32, (c_rows, sub), 0)
    sub_i = lax.broadcasted_iota(jnp.int32, (sub, sub), 0)
    sub_j = lax.broadcasted_iota(jnp.int32, (sub, sub), 1)
    sub_r = lax.broadcasted_iota(jnp.int32, (sub, 1), 0)

    def chunk(c, carry):
        r0 = pl.multiple_of(c * c_rows, c_rows)
        q = q_ref[pl.ds(r0, c_rows), :].astype(F32) * (GLA_DK ** -0.5)
        k = k_ref[pl.ds(r0, c_rows), :].astype(F32)
        v = v_ref[pl.ds(r0, c_rows), :].astype(F32)
        pre = jnp.dot(ga_ref[pl.ds(r0, c_rows), :], wup_ref[...],
                      preferred_element_type=F32,
                      precision=lax.Precision.HIGHEST) + bup_ref[...]
        g = jax.nn.log_sigmoid(pre) * (1.0 / GLA_GATE_NORMALIZER)
        b = jnp.dot(tril, g, preferred_element_type=F32,
                    precision=lax.Precision.HIGHEST)
        st = state_ref[...]
        o = lax.dot_general(q * jnp.exp(b), st, (((1,), (1,)), ((), ())),
                            preferred_element_type=F32,
                            precision=lax.Precision.HIGHEST)
        for j in range(n_sub):
            lo, hi = j * sub, (j + 1) * sub
            b_j = b[lo:hi, :]
            k_j = k[lo:hi, :]
            q_j = q[lo:hi, :]
            v_j = v[lo:hi, :]
            b_end = b[hi - 1:hi, :]
            diag = jnp.zeros((sub, sub), F32)
            for jj in range(sub):
                dec = jnp.exp(jnp.minimum(b_j - b_j[jj:jj + 1, :], 0.0))
                colv = jnp.sum(q_j * k_j[jj:jj + 1, :] * dec, axis=-1, keepdims=True)
                colv = jnp.where(sub_r >= jj, colv, 0.0)
                diag = jnp.where(sub_j == jj, colv, diag)
            pieces = []
            if lo > 0:
                pieces.append(jnp.zeros((lo, sub), F32))
            pieces.append(diag)
            if hi < c_rows:
                k_t = k_j * jnp.exp(b_end - b_j)
                q_t = q * jnp.exp(jnp.minimum(b - b_end, 0.0))
                s_off = lax.dot_general(q_t, k_t, (((1,), (1,)), ((), ())),
                                        preferred_element_type=F32,
                                        precision=lax.Precision.HIGHEST)
                s_off = jnp.where(row_s >= hi, s_off, 0.0)
                pieces.append(jnp.zeros((c_rows - hi, sub), F32))
                s_col = jnp.concatenate(pieces, axis=0) + s_off
            else:
                s_col = jnp.concatenate(pieces, axis=0)
            o = o + jnp.dot(s_col, v_j, preferred_element_type=F32,
                            precision=lax.Precision.HIGHEST)
        b_last = b[c_rows - 1:c_rows, :]
        k_s = k * jnp.exp(b_last - b)
        upd = lax.dot_general(v, k_s, (((0,), (0,)), ((), ())),
                              preferred_element_type=F32,
                              precision=lax.Precision.HIGHEST)
        state_ref[...] = st * jnp.exp(b_last) + upd
        ms = jnp.mean(o * o, axis=-1, keepdims=True)
        on = o * lax.rsqrt(ms + EPS) * gain_ref[...]
        r = r_ref[pl.ds(r0, c_rows), :].astype(F32)
        o_ref[pl.ds(r0, c_rows), :] = (on * (r * jax.nn.sigmoid(r))).astype(o_ref.dtype)
        return carry

    lax.fori_loop(0, n_chunks, chunk, 0)


def _gla(qk, gv, gr, ga, wup, bup, out_gain, batch, seq, ts):
    h, dk, dv = GLA_HEADS, GLA_DK, GLA_DV
    qk3 = qk.reshape(batch, seq, 2 * h * dk)
    gv3 = gv.reshape(batch, seq, h * dv)
    gr3 = gr.reshape(batch, seq, h * dv)
    ga3 = ga.reshape(batch, seq, LANES)
    kern = functools.partial(_gla_kernel, n_chunks=ts // GLA_CHUNK)
    out = pl.pallas_call(
        kern,
        grid=(batch, h, seq // ts),
        in_specs=[
            pl.BlockSpec((None, ts, dk), lambda b, hh, s: (b, s, hh)),
            pl.BlockSpec((None, ts, dk), lambda b, hh, s: (b, s, h + hh)),
            pl.BlockSpec((None, ts, dv), lambda b, hh, s: (b, s, hh)),
            pl.BlockSpec((None, ts, dv), lambda b, hh, s: (b, s, hh)),
            pl.BlockSpec((None, ts, LANES), lambda b, hh, s: (b, s, 0)),
            pl.BlockSpec((LANES, dk), lambda b, hh, s: (0, hh)),
            pl.BlockSpec((1, dk), lambda b, hh, s: (0, hh)),
            _const_spec((1, dv)),
        ],
        out_specs=pl.BlockSpec((None, ts, dv), lambda b, hh, s: (b, s, hh)),
        out_shape=jax.ShapeDtypeStruct((batch, seq, h * dv), BF16),
        scratch_shapes=[pltpu.VMEM((dv, dk), F32)],
        compiler_params=_cparams(("parallel", "parallel", "arbitrary")),
        name="gla",
    )(qk3, qk3, gv3, gr3, ga3, wup, bup, out_gain)
    return out.reshape(batch * seq, h * dv)


def _t5_bucket_table():
    w = SWA_WINDOW
    qi = np.arange(w)[:, None]
    kj = np.arange(2 * w)[None, :]
    rel = np.clip(qi + w - kj, 0, None)
    max_exact = REL_BUCKETS // 2
    rel_f = np.maximum(rel, 1).astype(np.float32)
    large = max_exact + (np.log(rel_f / np.float32(max_exact))
                         / np.float32(math.log(REL_MAX_DISTANCE / max_exact))
                         * np.float32(REL_BUCKETS - max_exact)).astype(np.int32)
    large = np.minimum(large, REL_BUCKETS - 1)
    return np.where(rel < max_exact, rel, large).astype(np.int32)


def _swa_kernel(q_ref, kp_ref, kc_ref, vp_ref, vc_ref, bias_ref, qg_ref, kg_ref,
                sink_ref, o_ref):
    w, hd = SWA_WINDOW, SWA_HEAD_DIM
    group = SWA_Q_HEADS // SWA_KV_HEADS
    n = pl.program_id(1)
    qi = lax.broadcasted_iota(jnp.int32, (w, 2 * w), 0)
    kj = lax.broadcasted_iota(jnp.int32, (w, 2 * w), 1)
    rel = qi + w - kj
    valid = (rel >= 0) & (rel < w) & ((kj >= w) | (n > 0))

    def norm(t, gain):
        return t * lax.rsqrt(jnp.mean(t * t, axis=-1, keepdims=True) + EPS) * gain

    q_all = q_ref[...].astype(F32)
    k_all = jnp.concatenate([kp_ref[...], kc_ref[...]], axis=0).astype(F32)
    v_all = jnp.concatenate([vp_ref[...], vc_ref[...]], axis=0)
    outs = []
    for g in range(SWA_KV_HEADS):
        kn = norm(k_all[:, g * hd:(g + 1) * hd], kg_ref[...]).astype(BF16)
        vb = v_all[:, g * hd:(g + 1) * hd]
        for j in range(group):
            hq = g * group + j
            qn = (norm(q_all[:, hq * hd:(hq + 1) * hd], qg_ref[...]) * (hd ** -0.5)).astype(BF16)
            logits = lax.dot_general(qn, kn, (((1,), (1,)), ((), ())),
                                     preferred_element_type=F32) + bias_ref[hq]
            logits = jnp.where(valid, logits, -jnp.inf)
            sink = sink_ref[0:1, hq:hq + 1]
            m = jnp.maximum(jnp.max(logits, axis=-1, keepdims=True), sink)
            e = jnp.exp(logits - m)
            denom = jnp.sum(e, axis=-1, keepdims=True) + jnp.exp(sink - m)
            pv = jnp.dot(e.astype(BF16), vb, preferred_element_type=F32)
            outs.append(pv / denom)
    o_ref[...] = jnp.concatenate(outs, axis=-1).astype(o_ref.dtype)


def _swa(sq, skv, bias, q_gain, k_gain, sinks, batch, seq):
    w, hd = SWA_WINDOW, SWA_HEAD_DIM
    hq, hkv = SWA_Q_HEADS, SWA_KV_HEADS
    nb = seq // w
    sq3 = sq.reshape(batch, seq, hq * hd)
    skv3 = skv.reshape(batch, seq, 2 * hkv * hd)
    prev = lambda b, n: (b, jnp.maximum(n - 1, 0), 0)
    prev_v = lambda b, n: (b, jnp.maximum(n - 1, 0), 1)
    out = pl.pallas_call(
        _swa_kernel,
        grid=(batch, nb),
        in_specs=[
            pl.BlockSpec((None, w, hq * hd), lambda b, n: (b, n, 0)),
            pl.BlockSpec((None, w, hkv * hd), prev),
            pl.BlockSpec((None, w, hkv * hd), lambda b, n: (b, n, 0)),
            pl.BlockSpec((None, w, hkv * hd), prev_v),
            pl.BlockSpec((None, w, hkv * hd), lambda b, n: (b, n, 1)),
            _const_spec((hq, w, 2 * w)),
            _const_spec((1, hd)),
            _const_spec((1, hd)),
            _const_spec((1, hq)),
        ],
        out_specs=pl.BlockSpec((None, w, hq * hd), lambda b, n: (b, n, 0)),
        out_shape=jax.ShapeDtypeStruct((batch, seq, hq * hd), BF16),
        compiler_params=_cparams(("parallel", "parallel")),
        name="swa",
    )(sq3, skv3, skv3, skv3, skv3, bias, q_gain, k_gain, sinks)
    return out.reshape(batch * seq, hq * hd)


def _merge_kernel(x_ref, ya_ref, yb_ref, gt_ref, wa_ref, wb_ref, wo_ref, gain_ref,
                  wq_ref, sk_ref, h_ref, hn_ref, sc_ref):
    d = x_ref.shape[1]
    pa = jnp.dot(ya_ref[...], wa_ref[...], preferred_element_type=F32)
    pb = jnp.dot(yb_ref[...], wb_ref[...], preferred_element_type=F32)
    gates = gt_ref[...].astype(F32)
    merged = jax.nn.sigmoid(gates[:, :d]) * pa + jax.nn.sigmoid(gates[:, d:]) * pb
    h = x_ref[...] + jnp.dot(merged.astype(BF16), wo_ref[...], preferred_element_type=F32)
    h_ref[...] = h
    ms = jnp.mean(h * h, axis=-1, keepdims=True)
    hn = (h * lax.rsqrt(ms + EPS) * gain_ref[...]).astype(BF16)
    hn_ref[...] = hn
    q = jnp.dot(hn, wq_ref[...], preferred_element_type=F32)
    hd = PEER_HALF_DIM
    for hp in range(2 * PEER_HEADS):
        sc_ref[hp] = lax.dot_general(sk_ref[hp], q[:, hp * hd:(hp + 1) * hd],
                                     (((1,), (1,)), ((), ())),
                                     preferred_element_type=F32,
                                     precision=lax.Precision.HIGHEST)


def _merge(x2, ya, yb, gates, wa, wb, wo, gain, wq, sk, tm):
    t, d = x2.shape
    nq = wq.shape[1]
    nhp, nk, hd = sk.shape
    tok = lambda w: pl.BlockSpec((tm, w), lambda i: (i, 0))
    return pl.pallas_call(
        _merge_kernel,
        grid=(t // tm,),
        in_specs=[tok(d), tok(ya.shape[1]), tok(yb.shape[1]), tok(2 * d),
                  _const_spec(wa.shape), _const_spec(wb.shape), _const_spec(wo.shape),
                  _const_spec((1, d)), _const_spec((d, nq)), _const_spec((nhp, nk, hd))],
        out_specs=[tok(d), tok(d), pl.BlockSpec((nhp, nk, tm), lambda i: (0, 0, i))],
        out_shape=[jax.ShapeDtypeStruct((t, d), F32),
                   jax.ShapeDtypeStruct((t, d), BF16),
                   jax.ShapeDtypeStruct((nhp, nk, t), F32)],
        compiler_params=_cparams(("parallel",)),
        name="merge",
    )(x2, ya, yb, gates, wa, wb, wo, gain, wq, sk)


def _top_values(s, k):
    t = s.shape[1]
    row = lax.broadcasted_iota(jnp.int32, (k, t), 0)
    vals = jnp.zeros((k, t), F32)
    cur = s
    for r in range(k):
        m = jnp.max(cur, axis=0, keepdims=True)
        vals = jnp.where(row == r, m, vals)
        if r + 1 < k:
            cur = jnp.where(cur == m, -jnp.inf, cur)
    return vals


def _peer_select_kernel(sc_ref, thr_ref, c_ref, e2_ref):
    kk = PEER_TOPK
    for h in range(PEER_HEADS):
        s1 = sc_ref[2 * h]
        s2 = sc_ref[2 * h + 1]
        t1 = _top_values(s1, kk)
        t2 = _top_values(s2, kk)
        cand = jnp.concatenate([t1[a:a + 1, :] + t2 for a in range(kk)], axis=0)
        tau = _top_values(cand, kk)[kk - 1:kk, :]
        m1 = t1[0:1, :]
        m2 = t2[0:1, :]
        z = jnp.sum(jnp.where(cand >= tau, jnp.exp(cand - (m1 + m2)), 0.0),
                    axis=0, keepdims=True)
        thr = jnp.full(s1.shape, jnp.inf, F32)
        for b in range(kk):
            t2b = t2[b:b + 1, :]
            thr = jnp.where(s1 + t2b >= tau, t2b, thr)
        thr_ref[h] = thr
        c_ref[h] = jnp.exp(s1 - m1) / z
        e2_ref[h] = jnp.exp(s2 - m2)


def _peer_select(scores_t, tn):
    nhp, nk, t = scores_t.shape
    nh = nhp // 2
    out_spec = pl.BlockSpec((nh, nk, tn), lambda i: (0, 0, i))
    out_sds = jax.ShapeDtypeStruct((nh, nk, t), F32)
    return pl.pallas_call(
        _peer_select_kernel,
        grid=(t // tn,),
        in_specs=[pl.BlockSpec((nhp, nk, tn), lambda i: (0, 0, i))],
        out_specs=[out_spec, out_spec, out_spec],
        out_shape=[out_sds, out_sds, out_sds],
        compiler_params=_cparams(("parallel",)),
        name="peer_select",
    )(scores_t)


def _peer_dense_kernel(hn_ref, h_ref, dn_ref, upt_ref, sc_ref, thr_ref, c_ref, e2_ref,
                       o_ref, act_ref, w_ref, acc_ref, *, rows_per_block):
    e = pl.program_id(1)
    nk = PEER_N_KEYS
    tm = hn_ref.shape[0]

    @pl.when(e == 0)
    def _():
        acc_ref[...] = jnp.zeros_like(acc_ref)

    act_ref[...] = lax.dot_general(dn_ref[...], hn_ref[...], (((1,), (1,)), ((), ())),
                                   preferred_element_type=F32)

    def token_chunk(tc, carry):
        lanes = pl.ds(pl.multiple_of(tc * LANES, LANES), LANES)
        for ii in range(rows_per_block):
            gate = jnp.zeros((nk, LANES), F32)
            for h in range(PEER_HEADS):
                thr = thr_ref[h, ii:ii + 1, lanes]
                coef = c_ref[h, ii:ii + 1, lanes]
                s2 = sc_ref[2 * h + 1, :, lanes]
                gate = gate + jnp.where(s2 >= thr, e2_ref[h, :, lanes] * coef, 0.0)
            a = act_ref[ii * nk:(ii + 1) * nk, lanes]
            gelu = 0.5 * a * (1.0 + lax.erf(a * (2.0 ** -0.5)))
            w_ref[ii * nk:(ii + 1) * nk, lanes] = (gate * gelu).astype(BF16)
        return carry

    lax.fori_loop(0, tm // LANES, token_chunk, 0)

    acc_ref[...] += jnp.dot(upt_ref[...], w_ref[...], preferred_element_type=F32)

    @pl.when(e == pl.num_programs(1) - 1)
    def _():
        o_ref[...] = h_ref[...] + acc_ref[...].T


def _peer_dense(hn, h, dn, upt, scores_t, thr, coef, e2, tm, eb):
    t, d = hn.shape
    ne = dn.shape[0]
    nk = PEER_N_KEYS
    nh = PEER_HEADS
    rows = eb // nk
    kern = functools.partial(_peer_dense_kernel, rows_per_block=rows)
    return pl.pallas_call(
        kern,
        grid=(t // tm, ne // eb),
        in_specs=[
            pl.BlockSpec((tm, d), lambda i, e: (i, 0)),
            pl.BlockSpec((tm, d), lambda i, e: (i, 0)),
            pl.BlockSpec((eb, d), lambda i, e: (e, 0)),
            pl.BlockSpec((d, eb), lambda i, e: (0, e)),
            pl.BlockSpec((2 * nh, nk, tm), lambda i, e: (0, 0, i)),
            pl.BlockSpec((nh, rows, tm), lambda i, e: (0, e, i)),
            pl.BlockSpec((nh, rows, tm), lambda i, e: (0, e, i)),
            pl.BlockSpec((nh, nk, tm), lambda i, e: (0, 0, i)),
        ],
        out_specs=pl.BlockSpec((tm, d), lambda i, e: (i, 0)),
        out_shape=jax.ShapeDtypeStruct((t, d), F32),
        scratch_shapes=[pltpu.VMEM((eb, tm), F32),
                        pltpu.VMEM((eb, tm), BF16),
                        pltpu.VMEM((d, tm), F32)],
        compiler_params=_cparams(("parallel", "arbitrary")),
        name="peer_dense",
    )(hn, h, dn, upt, scores_t, thr, coef, e2)


def _layer(h2, batch, seq, p):
    t, d = h2.shape
    hk, hv = GLA_HEADS * GLA_DK, GLA_HEADS * GLA_DV
    nq, nkv = SWA_Q_HEADS * SWA_HEAD_DIM, SWA_KV_HEADS * SWA_HEAD_DIM
    w_in = p["w_in"]
    o = 0
    w_qk = w_in[:, o:o + 2 * hk]; o += 2 * hk
    w_v = w_in[:, o:o + hv]; o += hv
    w_r = w_in[:, o:o + hv]; o += hv
    w_ga = jnp.pad(w_in[:, o:o + GLA_GATE_RANK], ((0, 0), (0, LANES - GLA_GATE_RANK))); o += GLA_GATE_RANK
    w_sq = w_in[:, o:o + nq]; o += nq
    w_skv = w_in[:, o:o + 2 * nkv]; o += 2 * nkv
    w_gt = w_in[:, o:o + 2 * d]; o += 2 * d
    assert o == w_in.shape[1]
    w_cat = jnp.concatenate([w_qk, w_v, w_r, w_ga, w_sq, w_skv, w_gt], axis=1).astype(BF16)
    widths = (2 * hk, hv, hv, LANES, nq, 2 * nkv, 2 * d)
    dtypes = (BF16, BF16, BF16, F32, BF16, BF16, BF16)
    tm = min(512, t)
    qk, gv, gr, ga, sq, skv, gates = _in_proj(
        h2, p["norm_mix_gain"].reshape(1, d), w_cat, widths, dtypes, tm)

    wup = jnp.pad(p["w_gate_up"], ((0, LANES - GLA_GATE_RANK), (0, 0)))
    y_gla = _gla(qk, gv, gr, ga, wup, p["b_gate_up"].reshape(1, hk),
                 p["gla_out_gain"].reshape(1, GLA_DV), batch, seq, min(512, seq))

    bias = jnp.transpose(p["rel_bias"][_t5_bucket_table()], (2, 0, 1)).astype(F32)
    y_swa = _swa(sq, skv, bias, p["swa_q_gain"].reshape(1, SWA_HEAD_DIM),
                 p["swa_k_gain"].reshape(1, SWA_HEAD_DIM),
                 p["swa_sinks"].reshape(1, SWA_Q_HEADS), batch, seq)

    sk = p["peer_sub_keys"].reshape(2 * PEER_HEADS, PEER_N_KEYS, PEER_HALF_DIM)
    h_mid, hn, scores_t = _merge(
        h2, y_gla, y_swa, gates, p["w_branch_gla"].astype(BF16),
        p["w_branch_swa"].astype(BF16), p["w_out"].astype(BF16),
        p["norm_ffn_gain"].reshape(1, d), p["peer_w_query"].astype(BF16), sk, min(256, t))

    thr, coef, e2 = _peer_select(scores_t, min(256, t))
    dn = p["peer_expert_down"].astype(BF16)
    upt = p["peer_expert_up"].astype(BF16).T
    return _peer_dense(hn, h_mid, dn, upt, scores_t, thr, coef, e2, tm, 1024)


def kernel(x, norm_mix_gain, w_in, w_gate_up, b_gate_up, gla_out_gain, swa_q_gain,
           swa_k_gain, swa_sinks, rel_bias, w_branch_gla, w_branch_swa, w_out,
           norm_ffn_gain, peer_w_query, peer_sub_keys, peer_expert_down, peer_expert_up):
    batch, seq, d = x.shape
    h2 = x.reshape(batch * seq, d)
    for layer in range(w_in.shape[0]):
        p = dict(norm_mix_gain=norm_mix_gain[layer], w_in=w_in[layer],
                 w_gate_up=w_gate_up[layer], b_gate_up=b_gate_up[layer],
                 gla_out_gain=gla_out_gain[layer], swa_q_gain=swa_q_gain[layer],
                 swa_k_gain=swa_k_gain[layer], swa_sinks=swa_sinks[layer],
                 rel_bias=rel_bias, w_branch_gla=w_branch_gla[layer],
                 w_branch_swa=w_branch_swa[layer], w_out=w_out[layer],
                 norm_ffn_gain=norm_ffn_gain[layer], peer_w_query=peer_w_query[layer],
                 peer_sub_keys=peer_sub_keys[layer], peer_expert_down=peer_expert_down[layer],
                 peer_expert_up=peer_expert_up[layer])
        h2 = _layer(h2, batch, seq, p)
    return h2.reshape(batch, seq, d)
```

```python
import functools
import math

import numpy as np
import jax
import jax.numpy as jnp
from jax import lax
from jax.experimental import pallas as pl
from jax.experimental.pallas import tpu as pltpu

F32 = jnp.float32
BF16 = jnp.bfloat16

EPS = 1e-6

GLA_HEADS = 4
GLA_DK = 128
GLA_DV = 256
GLA_GATE_RANK = 16
GLA_GATE_NORMALIZER = 16.0
GLA_CHUNK = 64
GLA_SUB = 16

SWA_Q_HEADS = 16
SWA_KV_HEADS = 4
SWA_HEAD_DIM = 64
SWA_WINDOW = 128
REL_BUCKETS = 32
REL_MAX_DISTANCE = 128

PEER_HEADS = 8
PEER_N_KEYS = 128
PEER_HALF_DIM = 128
PEER_TOPK = 16

LANES = 128
VMEM_LIMIT = 56 * 1024 * 1024


def _cparams(semantics):
    return pltpu.CompilerParams(dimension_semantics=semantics,
                                vmem_limit_bytes=VMEM_LIMIT)


def _const_spec(shape):
    nd = len(shape)
    return pl.BlockSpec(shape, lambda *_: (0,) * nd)


def _in_proj_kernel(x_ref, g_ref, w_ref, *out_refs):
    x = x_ref[...]
    ms = jnp.mean(x * x, axis=-1, keepdims=True)
    xn = (x * lax.rsqrt(ms + EPS) * g_ref[...]).astype(BF16)
    off = 0
    for o_ref in out_refs:
        width = o_ref.shape[1]
        o_ref[...] = jnp.dot(xn, w_ref[:, off:off + width],
                             preferred_element_type=F32).astype(o_ref.dtype)
        off += width


def _in_proj(x2, gain, w_cat, widths, dtypes, tm):
    t, d = x2.shape
    nw = w_cat.shape[1]
    out_shape = [jax.ShapeDtypeStruct((t, w), dt) for w, dt in zip(widths, dtypes)]
    out_specs = [pl.BlockSpec((tm, w), lambda i: (i, 0)) for w in widths]
    return pl.pallas_call(
        _in_proj_kernel,
        grid=(t // tm,),
        in_specs=[pl.BlockSpec((tm, d), lambda i: (i, 0)),
                  _const_spec((1, d)),
                  _const_spec((d, nw))],
        out_specs=out_specs,
        out_shape=out_shape,
        compiler_params=_cparams(("parallel",)),
        name="in_proj",
    )(x2, gain, w_cat)


def _gla_kernel(q_ref, k_ref, v_ref, r_ref, ga_ref, wup_ref, bup_ref, gain_ref,
                o_ref, state_ref, *, n_chunks):
    c_rows, sub = GLA_CHUNK, GLA_SUB
    n_sub = c_rows // sub

    @pl.when(pl.program_id(2) == 0)
    def _():
        state_ref[...] = jnp.zeros_like(state_ref)

    row = lax.broadcasted_iota(jnp.int32, (c_rows, c_rows), 0)
    col = lax.broadcasted_iota(jnp.int32, (c_rows, c_rows), 1)
    tril = (row >= col).astype(F32)
    row_s = lax.broadcasted_iota(jnp.int32, (c_rows, sub), 0)
    sub_i = lax.broadcasted_iota(jnp.int32, (sub, sub), 0)
    sub_j = lax.broadcasted_iota(jnp.int32, (sub, sub), 1)
    sub_r = lax.broadcasted_iota(jnp.int32, (sub, 1), 0)

    def chunk(c, carry):
        r0 = pl.multiple_of(c * c_rows, c_rows)
        q = q_ref[pl.ds(r0, c_rows), :].astype(F32) * (GLA_DK ** -0.5)
        k = k_ref[pl.ds(r0, c_rows), :].astype(F32)
        v = v_ref[pl.ds(r0, c_rows), :].astype(F32)
        pre = jnp.dot(ga_ref[pl.ds(r0, c_rows), :], wup_ref[...],
                      preferred_element_type=F32,
                      precision=lax.Precision.HIGHEST) + bup_ref[...]
        g = jax.nn.log_sigmoid(pre) * (1.0 / GLA_GATE_NORMALIZER)
        b = jnp.dot(tril, g, preferred_element_type=F32,
                    precision=lax.Precision.HIGHEST)
        st = state_ref[...]
        o = lax.dot_general(q * jnp.exp(b), st, (((1,), (1,)), ((), ())),
                            preferred_element_type=F32,
                            precision=lax.Precision.HIGHEST)
        for j in range(n_sub):
            lo, hi = j * sub, (j + 1) * sub
            b_j = b[lo:hi, :]
            k_j = k[lo:hi, :]
            q_j = q[lo:hi, :]
            v_j = v[lo:hi, :]
            b_end = b[hi - 1:hi, :]
            diag = jnp.zeros((sub, sub), F32)
            for jj in range(sub):
                dec = jnp.exp(jnp.minimum(b_j - b_j[jj:jj + 1, :], 0.0))
                colv = jnp.sum(q_j * k_j[jj:jj + 1, :] * dec, axis=-1, keepdims=True)
                colv = jnp.where(sub_r >= jj, colv, 0.0)
                diag = jnp.where(sub_j == jj, colv, diag)
            pieces = []
            if lo > 0:
                pieces.append(jnp.zeros((lo, sub), F32))
            pieces.append(diag)
            if hi < c_rows:
                k_t = k_j * jnp.exp(b_end - b_j)
                q_t = q * jnp.exp(jnp.minimum(b - b_end, 0.0))
                s_off = lax.dot_general(q_t, k_t, (((1,), (1,)), ((), ())),
                                        preferred_element_type=F32,
                                        precision=lax.Precision.HIGHEST)
                s_off = jnp.where(row_s >= hi, s_off, 0.0)
                pieces.append(jnp.zeros((c_rows - hi, sub), F32))
                s_col = jnp.concatenate(pieces, axis=0) + s_off
            else:
                s_col = jnp.concatenate(pieces, axis=0)
            o = o + jnp.dot(s_col, v_j, preferred_element_type=F32,
                            precision=lax.Precision.HIGHEST)
        b_last = b[c_rows - 1:c_rows, :]
        k_s = k * jnp.exp(b_last - b)
        upd = lax.dot_general(v, k_s, (((0,), (0,)), ((), ())),
                              preferred_element_type=F32,
                              precision=lax.Precision.HIGHEST)
        state_ref[...] = st * jnp.exp(b_last) + upd
        ms = jnp.mean(o * o, axis=-1, keepdims=True)
        on = o * lax.rsqrt(ms + EPS) * gain_ref[...]
        r = r_ref[pl.ds(r0, c_rows), :].astype(F32)
        o_ref[pl.ds(r0, c_rows), :] = (on * (r * jax.nn.sigmoid(r))).astype(o_ref.dtype)
        return carry

    lax.fori_loop(0, n_chunks, chunk, 0)


def _gla(qk, gv, gr, ga, wup, bup, out_gain, batch, seq, ts):
    h, dk, dv = GLA_HEADS, GLA_DK, GLA_DV
    qk3 = qk.reshape(batch, seq, 2 * h * dk)
    gv3 = gv.reshape(batch, seq, h * dv)
    gr3 = gr.reshape(batch, seq, h * dv)
    ga3 = ga.reshape(batch, seq, LANES)
    kern = functools.partial(_gla_kernel, n_chunks=ts // GLA_CHUNK)
    out = pl.pallas_call(
        kern,
        grid=(batch, h, seq // ts),
        in_specs=[
            pl.BlockSpec((None, ts, dk), lambda b, hh, s: (b, s, hh)),
            pl.BlockSpec((None, ts, dk), lambda b, hh, s: (b, s, h + hh)),
            pl.BlockSpec((None, ts, dv), lambda b, hh, s: (b, s, hh)),
            pl.BlockSpec((None, ts, dv), lambda b, hh, s: (b, s, hh)),
            pl.BlockSpec((None, ts, LANES), lambda b, hh, s: (b, s, 0)),
            pl.BlockSpec((LANES, dk), lambda b, hh, s: (0, hh)),
            pl.BlockSpec((1, dk), lambda b, hh, s: (0, hh)),
            _const_spec((1, dv)),
        ],
        out_specs=pl.BlockSpec((None, ts, dv), lambda b, hh, s: (b, s, hh)),
        out_shape=jax.ShapeDtypeStruct((batch, seq, h * dv), BF16),
        scratch_shapes=[pltpu.VMEM((dv, dk), F32)],
        compiler_params=_cparams(("parallel", "parallel", "arbitrary")),
        name="gla",
    )(qk3, qk3, gv3, gr3, ga3, wup, bup, out_gain)
    return out.reshape(batch * seq, h * dv)


def _t5_bucket_table():
    w = SWA_WINDOW
    qi = np.arange(w)[:, None]
    kj = np.arange(2 * w)[None, :]
    rel = np.clip(qi + w - kj, 0, None)
    max_exact = REL_BUCKETS // 2
    rel_f = np.maximum(rel, 1).astype(np.float32)
    large = max_exact + (np.log(rel_f / np.float32(max_exact))
                         / np.float32(math.log(REL_MAX_DISTANCE / max_exact))
                         * np.float32(REL_BUCKETS - max_exact)).astype(np.int32)
    large = np.minimum(large, REL_BUCKETS - 1)
    return np.where(rel < max_exact, rel, large).astype(np.int32)


def _swa_kernel(q_ref, kp_ref, kc_ref, vp_ref, vc_ref, bias_ref, qg_ref, kg_ref,
                sink_ref, o_ref):
    w, hd = SWA_WINDOW, SWA_HEAD_DIM
    group = SWA_Q_HEADS // SWA_KV_HEADS
    n = pl.program_id(1)
    qi = lax.broadcasted_iota(jnp.int32, (w, 2 * w), 0)
    kj = lax.broadcasted_iota(jnp.int32, (w, 2 * w), 1)
    rel = qi + w - kj
    valid = (rel >= 0) & (rel < w) & ((kj >= w) | (n > 0))

    def norm(t, gain):
        return t * lax.rsqrt(jnp.mean(t * t, axis=-1, keepdims=True) + EPS) * gain

    q_all = q_ref[...].astype(F32)
    k_all = jnp.concatenate([kp_ref[...], kc_ref[...]], axis=0).astype(F32)
    v_all = jnp.concatenate([vp_ref[...], vc_ref[...]], axis=0)
    outs = []
    for g in range(SWA_KV_HEADS):
        kn = norm(k_all[:, g * hd:(g + 1) * hd], kg_ref[...]).astype(BF16)
        vb = v_all[:, g * hd:(g + 1) * hd]
        for j in range(group):
            hq = g * group + j
            qn = (norm(q_all[:, hq * hd:(hq + 1) * hd], qg_ref[...]) * (hd ** -0.5)).astype(BF16)
            logits = lax.dot_general(qn, kn, (((1,), (1,)), ((), ())),
                                     preferred_element_type=F32) + bias_ref[hq]
            logits = jnp.where(valid, logits, -jnp.inf)
            sink = sink_ref[0:1, hq:hq + 1]
            m = jnp.maximum(jnp.max(logits, axis=-1, keepdims=True), sink)
            e = jnp.exp(logits - m)
            denom = jnp.sum(e, axis=-1, keepdims=True) + jnp.exp(sink - m)
            pv = jnp.dot(e.astype(BF16), vb, preferred_element_type=F32)
            outs.append(pv / denom)
    o_ref[...] = jnp.concatenate(outs, axis=-1).astype(o_ref.dtype)


def _swa(sq, skv, bias, q_gain, k_gain, sinks, batch, seq):
    w, hd = SWA_WINDOW, SWA_HEAD_DIM
    hq, hkv = SWA_Q_HEADS, SWA_KV_HEADS
    nb = seq // w
    sq3 = sq.reshape(batch, seq, hq * hd)
    skv3 = skv.reshape(batch, seq, 2 * hkv * hd)
    prev = lambda b, n: (b, jnp.maximum(n - 1, 0), 0)
    prev_v = lambda b, n: (b, jnp.maximum(n - 1, 0), 1)
    out = pl.pallas_call(
        _swa_kernel,
        grid=(batch, nb),
        in_specs=[
            pl.BlockSpec((None, w, hq * hd), lambda b, n: (b, n, 0)),
            pl.BlockSpec((None, w, hkv * hd), prev),
            pl.BlockSpec((None, w, hkv * hd), lambda b, n: (b, n, 0)),
            pl.BlockSpec((None, w, hkv * hd), prev_v),
            pl.BlockSpec((None, w, hkv * hd), lambda b, n: (b, n, 1)),
            _const_spec((hq, w, 2 * w)),
            _const_spec((1, hd)),
            _const_spec((1, hd)),
            _const_spec((1, hq)),
        ],
        out_specs=pl.BlockSpec((None, w, hq * hd), lambda b, n: (b, n, 0)),
        out_shape=jax.ShapeDtypeStruct((batch, seq, hq * hd), BF16),
        compiler_params=_cparams(("parallel", "parallel")),
        name="swa",
    )(sq3, skv3, skv3, skv3, skv3, bias, q_gain, k_gain, sinks)
    return out.reshape(batch * seq, hq * hd)


def _merge_kernel(x_ref, ya_ref, yb_ref, gt_ref, wa_ref, wb_ref, wo_ref, gain_ref,
                  wq_ref, sk_ref, h_ref, hn_ref, sc_ref):
    d = x_ref.shape[1]
    pa = jnp.dot(ya_ref[...], wa_ref[...], preferred_element_type=F32)
    pb = jnp.dot(yb_ref[...], wb_ref[...], preferred_element_type=F32)
    gates = gt_ref[...].astype(F32)
    merged = jax.nn.sigmoid(gates[:, :d]) * pa + jax.nn.sigmoid(gates[:, d:]) * pb
    h = x_ref[...] + jnp.dot(merged.astype(BF16), wo_ref[...], preferred_element_type=F32)
    h_ref[...] = h
    ms = jnp.mean(h * h, axis=-1, keepdims=True)
    hn = (h * lax.rsqrt(ms + EPS) * gain_ref[...]).astype(BF16)
    hn_ref[...] = hn
    q = jnp.dot(hn, wq_ref[...], preferred_element_type=F32)
    hd = PEER_HALF_DIM
    for hp in range(2 * PEER_HEADS):
        sc_ref[hp] = lax.dot_general(sk_ref[hp], q[:, hp * hd:(hp + 1) * hd],
                                     (((1,), (1,)), ((), ())),
                                     preferred_element_type=F32,
                                     precision=lax.Precision.HIGHEST)


def _merge(x2, ya, yb, gates, wa, wb, wo, gain, wq, sk, tm):
    t, d = x2.shape
    nq = wq.shape[1]
    nhp, nk, hd = sk.shape
    tok = lambda w: pl.BlockSpec((tm, w), lambda i: (i, 0))
    return pl.pallas_call(
        _merge_kernel,
        grid=(t // tm,),
        in_specs=[tok(d), tok(ya.shape[1]), tok(yb.shape[1]), tok(2 * d),
                  _const_spec(wa.shape), _const_spec(wb.shape), _const_spec(wo.shape),
                  _const_spec((1, d)), _const_spec((d, nq)), _const_spec((nhp, nk, hd))],
        out_specs=[tok(d), tok(d), pl.BlockSpec((nhp, nk, tm), lambda i: (0, 0, i))],
        out_shape=[jax.ShapeDtypeStruct((t, d), F32),
                   jax.ShapeDtypeStruct((t, d), BF16),
                   jax.ShapeDtypeStruct((nhp, nk, t), F32)],
        compiler_params=_cparams(("parallel",)),
        name="merge",
    )(x2, ya, yb, gates, wa, wb, wo, gain, wq, sk)


def _top_values(s, k, with_rank=False):
    t = s.shape[1]
    row = lax.broadcasted_iota(jnp.int32, (k, t), 0)
    vals = jnp.zeros((k, t), F32)
    rank = jnp.full(s.shape, float(k), F32) if with_rank else None
    cur = s
    for r in range(k):
        m = jnp.max(cur, axis=0, keepdims=True)
        vals = jnp.where(row == r, m, vals)
        hit = cur == m
        if with_rank:
            rank = jnp.where(hit, float(r), rank)
        if r + 1 < k:
            cur = jnp.where(hit, -jnp.inf, cur)
    return vals, rank


def _candidate_sums(t1, t2, k):
    assert k == 16
    sub = 8
    row = lax.broadcasted_iota(jnp.int32, (sub, t1.shape[1]), 0)
    pieces = [t1[0:1, :] + t2]
    for a in range(1, sub):
        pieces.append(jnp.where(row < k // (a + 1), t1[a:a + 1, :] + t2[0:sub, :], -jnp.inf))
    pieces.append(t1[sub:k, :] + t2[0:1, :])
    return jnp.concatenate(pieces, axis=0)


def _peer_select_kernel(sc_ref, rank_ref, e2_ref, n_ref, c_ref):
    kk = PEER_TOPK
    for h in range(PEER_HEADS):
        s1 = sc_ref[2 * h]
        s2 = sc_ref[2 * h + 1]
        t1, _ = _top_values(s1, kk)
        t2, rank2 = _top_values(s2, kk, with_rank=True)
        cand = _candidate_sums(t1, t2, kk)
        tau = _top_values(cand, kk)[0][kk - 1:kk, :]
        m1 = t1[0:1, :]
        m2 = t2[0:1, :]
        z = jnp.sum(jnp.where(cand >= tau, jnp.exp(cand - (m1 + m2)), 0.0),
                    axis=0, keepdims=True)
        count = jnp.zeros(s1.shape, F32)
        for b in range(kk):
            count = count + jnp.where(s1 + t2[b:b + 1, :] >= tau, 1.0, 0.0)
        rank_ref[h] = pltpu.bitcast(rank2.astype(BF16), jnp.uint32)
        e2_ref[h] = pltpu.bitcast(jnp.exp(s2 - m2).astype(BF16), jnp.uint32)
        n_ref[h] = count
        c_ref[h] = jnp.exp(s1 - m1) / z


def _peer_select(scores_t, tn):
    nhp, nk, t = scores_t.shape
    nh = nhp // 2
    out_spec = pl.BlockSpec((nh, nk, tn), lambda i: (0, 0, i))
    packed_spec = pl.BlockSpec((nh, nk // 2, tn), lambda i: (0, 0, i))
    return pl.pallas_call(
        _peer_select_kernel,
        grid=(t // tn,),
        in_specs=[pl.BlockSpec((nhp, nk, tn), lambda i: (0, 0, i))],
        out_specs=[packed_spec, packed_spec, out_spec, out_spec],
        out_shape=[jax.ShapeDtypeStruct((nh, nk // 2, t), jnp.uint32),
                   jax.ShapeDtypeStruct((nh, nk // 2, t), jnp.uint32),
                   jax.ShapeDtypeStruct((nh, nk, t), F32),
                   jax.ShapeDtypeStruct((nh, nk, t), F32)],
        compiler_params=_cparams(("parallel",)),
        name="peer_select",
    )(scores_t)


def _gated_activation(rank_ref, e2_ref, n_ref, c_ref, act_ref, w_ref, row0, n_rows):
    nk = PEER_N_KEYS
    half = nk // 2
    pack = 16
    tm = act_ref.shape[1]
    for tc in range(tm // LANES):
        lanes = slice(tc * LANES, (tc + 1) * LANES)
        for hf in range(2):
            words = slice(hf * half // 2, (hf + 1) * half // 2)
            gates = [None] * n_rows
            for h in range(PEER_HEADS):
                rk = pltpu.bitcast(rank_ref[h, words, lanes], BF16)
                ev = pltpu.bitcast(e2_ref[h, words, lanes], BF16)
                for ii in range(n_rows):
                    i1 = row0 + ii
                    cnt = jnp.broadcast_to(n_ref[h, i1:i1 + 1, lanes], (pack, LANES)).astype(BF16)
                    cf = jnp.broadcast_to(c_ref[h, i1:i1 + 1, lanes], (pack, LANES)).astype(BF16)
                    cnt = jnp.concatenate([cnt] * (half // pack), axis=0)
                    cf = jnp.concatenate([cf] * (half // pack), axis=0)
                    term = jnp.where(rk < cnt, ev * cf, jnp.zeros_like(ev))
                    gates[ii] = term if gates[ii] is None else gates[ii] + term
            for ii in range(n_rows):
                rows = slice(ii * nk + hf * half, ii * nk + (hf + 1) * half)
                a = act_ref[rows, lanes]
                gelu = 0.5 * a * (1.0 + lax.erf(a * (2.0 ** -0.5)))
                w_ref[rows, lanes] = gates[ii] * gelu.astype(BF16)


def _peer_dense_kernel(hn_ref, h_ref, dn_ref, upt_ref, rank_ref, e2_ref, n_ref, c_ref,
                       o_ref, act_ref, w_ref, acc_ref):
    e = pl.program_id(1)
    n_split, ebs, _ = act_ref.shape
    rows = ebs // PEER_N_KEYS

    @pl.when(e == 0)
    def _():
        acc_ref[...] = jnp.zeros_like(acc_ref)

    hn = hn_ref[...]
    for s in range(n_split):
        act_ref[s] = lax.dot_general(dn_ref[s * ebs:(s + 1) * ebs, :], hn,
                                     (((1,), (1,)), ((), ())), preferred_element_type=F32)
    for s in range(n_split):
        _gated_activation(rank_ref, e2_ref, n_ref, c_ref, act_ref.at[s], w_ref.at[s],
                          s * rows, rows)
    for s in range(n_split):
        acc_ref[...] += jnp.dot(upt_ref[:, s * ebs:(s + 1) * ebs], w_ref[s],
                                preferred_element_type=F32)

    @pl.when(e == pl.num_programs(1) - 1)
    def _():
        o_ref[...] = h_ref[...] + acc_ref[...].T


def _peer_dense(hn, h, dn, upt, rank2, e2, count, coef, tm, eb, n_split):
    t, d = hn.shape
    ne = dn.shape[0]
    nk = PEER_N_KEYS
    nh = PEER_HEADS
    rows = eb // nk
    ebs = eb // n_split
    return pl.pallas_call(
        _peer_dense_kernel,
        grid=(t // tm, ne // eb),
        in_specs=[
            pl.BlockSpec((tm, d), lambda i, e: (i, 0)),
            pl.BlockSpec((tm, d), lambda i, e: (i, 0)),
            pl.BlockSpec((eb, d), lambda i, e: (e, 0)),
            pl.BlockSpec((d, eb), lambda i, e: (0, e)),
            pl.BlockSpec((nh, nk // 2, tm), lambda i, e: (0, 0, i)),
            pl.BlockSpec((nh, nk // 2, tm), lambda i, e: (0, 0, i)),
            pl.BlockSpec((nh, rows, tm), lambda i, e: (0, e, i)),
            pl.BlockSpec((nh, rows, tm), lambda i, e: (0, e, i)),
        ],
        out_specs=pl.BlockSpec((tm, d), lambda i, e: (i, 0)),
        out_shape=jax.ShapeDtypeStruct((t, d), F32),
        scratch_shapes=[pltpu.VMEM((n_split, ebs, tm), F32),
                        pltpu.VMEM((n_split, ebs, tm), BF16),
                        pltpu.VMEM((d, tm), F32)],
        compiler_params=_cparams(("parallel", "arbitrary")),
        name="peer_dense",
    )(hn, h, dn, upt, rank2, e2, count, coef)


def _layer(h2, batch, seq, p):
    t, d = h2.shape
    hk, hv = GLA_HEADS * GLA_DK, GLA_HEADS * GLA_DV
    nq, nkv = SWA_Q_HEADS * SWA_HEAD_DIM, SWA_KV_HEADS * SWA_HEAD_DIM
    w_in = p["w_in"]
    o = 0
    w_qk = w_in[:, o:o + 2 * hk]; o += 2 * hk
    w_v = w_in[:, o:o + hv]; o += hv
    w_r = w_in[:, o:o + hv]; o += hv
    w_ga = jnp.pad(w_in[:, o:o + GLA_GATE_RANK], ((0, 0), (0, LANES - GLA_GATE_RANK))); o += GLA_GATE_RANK
    w_sq = w_in[:, o:o + nq]; o += nq
    w_skv = w_in[:, o:o + 2 * nkv]; o += 2 * nkv
    w_gt = w_in[:, o:o + 2 * d]; o += 2 * d
    assert o == w_in.shape[1]
    w_cat = jnp.concatenate([w_qk, w_v, w_r, w_ga, w_sq, w_skv, w_gt], axis=1).astype(BF16)
    widths = (2 * hk, hv, hv, LANES, nq, 2 * nkv, 2 * d)
    dtypes = (BF16, BF16, BF16, F32, BF16, BF16, BF16)
    tm = min(512, t)
    qk, gv, gr, ga, sq, skv, gates = _in_proj(
        h2, p["norm_mix_gain"].reshape(1, d), w_cat, widths, dtypes, tm)

    wup = jnp.pad(p["w_gate_up"], ((0, LANES - GLA_GATE_RANK), (0, 0)))
    y_gla = _gla(qk, gv, gr, ga, wup, p["b_gate_up"].reshape(1, hk),
                 p["gla_out_gain"].reshape(1, GLA_DV), batch, seq, min(512, seq))

    bias = jnp.transpose(p["rel_bias"][_t5_bucket_table()], (2, 0, 1)).astype(F32)
    y_swa = _swa(sq, skv, bias, p["swa_q_gain"].reshape(1, SWA_HEAD_DIM),
                 p["swa_k_gain"].reshape(1, SWA_HEAD_DIM),
                 p["swa_sinks"].reshape(1, SWA_Q_HEADS), batch, seq)

    sk = p["peer_sub_keys"].reshape(2 * PEER_HEADS, PEER_N_KEYS, PEER_HALF_DIM)
    h_mid, hn, scores_t = _merge(
        h2, y_gla, y_swa, gates, p["w_branch_gla"].astype(BF16),
        p["w_branch_swa"].astype(BF16), p["w_out"].astype(BF16),
        p["norm_ffn_gain"].reshape(1, d), p["peer_w_query"].astype(BF16), sk, min(256, t))

    rank2, e2, count, coef = _peer_select(scores_t, min(256, t))
    dn = p["peer_expert_down"].astype(BF16)
    upt = p["peer_expert_up"].astype(BF16).T
    return _peer_dense(hn, h_mid, dn, upt, rank2, e2, count, coef, tm, 2048, 2)


def kernel(x, norm_mix_gain, w_in, w_gate_up, b_gate_up, gla_out_gain, swa_q_gain,
           swa_k_gain, swa_sinks, rel_bias, w_branch_gla, w_branch_swa, w_out,
           norm_ffn_gain, peer_w_query, peer_sub_keys, peer_expert_down, peer_expert_up):
    batch, seq, d = x.shape
    h2 = x.reshape(batch * seq, d)
    for layer in range(w_in.shape[0]):
        p = dict(norm_mix_gain=norm_mix_gain[layer], w_in=w_in[layer],
                 w_gate_up=w_gate_up[layer], b_gate_up=b_gate_up[layer],
                 gla_out_gain=gla_out_gain[layer], swa_q_gain=swa_q_gain[layer],
                 swa_k_gain=swa_k_gain[layer], swa_sinks=swa_sinks[layer],
                 rel_bias=rel_bias, w_branch_gla=w_branch_gla[layer],
                 w_branch_swa=w_branch_swa[layer], w_out=w_out[layer],
                 norm_ffn_gain=norm_ffn_gain[layer], peer_w_query=peer_w_query[layer],
                 peer_sub_keys=peer_sub_keys[layer], peer_expert_down=peer_expert_down[layer],
                 peer_expert_up=peer_expert_up[layer])
        h2 = _layer(h2, batch, seq, p)
    return h2.reshape(batch, seq, d)
```

```python
import functools
import math

import numpy as np
import jax
import jax.numpy as jnp
from jax import lax
from jax.experimental import pallas as pl
from jax.experimental.pallas import tpu as pltpu

F32 = jnp.float32
BF16 = jnp.bfloat16

EPS = 1e-6

GLA_HEADS = 4
GLA_DK = 128
GLA_DV = 256
GLA_GATE_RANK = 16
GLA_GATE_NORMALIZER = 16.0
GLA_CHUNK = 64
GLA_SUB = 16

SWA_Q_HEADS = 16
SWA_KV_HEADS = 4
SWA_HEAD_DIM = 64
SWA_WINDOW = 128
REL_BUCKETS = 32
REL_MAX_DISTANCE = 128

PEER_HEADS = 8
PEER_N_KEYS = 128
PEER_HALF_DIM = 128
PEER_TOPK = 16

LANES = 128
VMEM_LIMIT = 56 * 1024 * 1024

NT_DIMS = (((1,), (1,)), ((), ()))
TN_DIMS = (((0,), (0,)), ((), ()))


def _cparams(semantics):
    return pltpu.CompilerParams(dimension_semantics=semantics,
                                vmem_limit_bytes=VMEM_LIMIT)


def _const_spec(shape):
    nd = len(shape)
    return pl.BlockSpec(shape, lambda *_: (0,) * nd)


def _split_bf16(x):
    hi = x.astype(BF16)
    lo = (x - hi.astype(F32)).astype(BF16)
    return hi, lo


def _in_proj_kernel(x_ref, g_ref, w_ref, wga_ref, wup_ref, bup_ref, *out_refs):
    x = x_ref[...]
    ms = jnp.mean(x * x, axis=-1, keepdims=True)
    xn = (x * lax.rsqrt(ms + EPS) * g_ref[...]).astype(BF16)
    off = 0
    for o_ref in out_refs[:-1]:
        width = o_ref.shape[1]
        o_ref[...] = jnp.dot(xn, w_ref[:, off:off + width],
                             preferred_element_type=F32).astype(o_ref.dtype)
        off += width
    a_low = jnp.dot(xn, wga_ref[...], preferred_element_type=F32)
    pre = jnp.dot(a_low, wup_ref[...], preferred_element_type=F32,
                  precision=lax.Precision.HIGHEST) + bup_ref[...]
    out_refs[-1][...] = jax.nn.log_sigmoid(pre) * (1.0 / GLA_GATE_NORMALIZER)


def _in_proj(x2, gain, w_cat, w_ga, w_up, b_up, widths, tm):
    t, d = x2.shape
    nw = w_cat.shape[1]
    ng = w_up.shape[1]
    out_shape = [jax.ShapeDtypeStruct((t, w), BF16) for w in widths]
    out_shape.append(jax.ShapeDtypeStruct((t, ng), F32))
    out_specs = [pl.BlockSpec((tm, w), lambda i: (i, 0)) for w in widths + (ng,)]
    return pl.pallas_call(
        _in_proj_kernel,
        grid=(t // tm,),
        in_specs=[pl.BlockSpec((tm, d), lambda i: (i, 0)),
                  _const_spec((1, d)),
                  _const_spec((d, nw)),
                  _const_spec(w_ga.shape),
                  _const_spec(w_up.shape),
                  _const_spec((1, ng))],
        out_specs=out_specs,
        out_shape=out_shape,
        compiler_params=_cparams(("parallel",)),
        name="in_proj",
    )(x2, gain, w_cat, w_ga, w_up, b_up)


def _gla_chunk_head(q, k, v, g, st, tril):
    c_rows, sub = GLA_CHUNK, GLA_SUB
    row_s = lax.broadcasted_iota(jnp.int32, (c_rows, sub), 0)
    sub_j = lax.broadcasted_iota(jnp.int32, (sub, sub), 1)
    sub_r = lax.broadcasted_iota(jnp.int32, (sub, 1), 0)
    g_hi, g_lo = _split_bf16(g)
    b = (jnp.dot(tril, g_hi, preferred_element_type=F32)
         + jnp.dot(tril, g_lo, preferred_element_type=F32))
    o = lax.dot_general((q * jnp.exp(b)).astype(BF16), st.astype(BF16), NT_DIMS,
                        preferred_element_type=F32)
    for j in range(c_rows // sub):
        lo, hi = j * sub, (j + 1) * sub
        b_j, k_j, q_j = b[lo:hi, :], k[lo:hi, :], q[lo:hi, :]
        b_end = b[hi - 1:hi, :]
        diag = jnp.zeros((sub, sub), F32)
        for jj in range(sub):
            dec = jnp.exp(jnp.minimum(b_j - b_j[jj:jj + 1, :], 0.0))
            colv = jnp.sum(q_j * k_j[jj:jj + 1, :] * dec, axis=-1, keepdims=True)
            colv = jnp.where(sub_r >= jj, colv, 0.0)
            diag = jnp.where(sub_j == jj, colv, diag)
        pieces = []
        if lo > 0:
            pieces.append(jnp.zeros((lo, sub), F32))
        pieces.append(diag)
        if hi < c_rows:
            pieces.append(jnp.zeros((c_rows - hi, sub), F32))
        s_col = jnp.concatenate(pieces, axis=0)
        if hi < c_rows:
            k_t = (k_j * jnp.exp(b_end - b_j)).astype(BF16)
            q_t = (q * jnp.exp(jnp.minimum(b - b_end, 0.0))).astype(BF16)
            s_off = lax.dot_general(q_t, k_t, NT_DIMS, preferred_element_type=F32)
            s_col = s_col + jnp.where(row_s >= hi, s_off, 0.0)
        o = o + jnp.dot(s_col.astype(BF16), v[lo:hi, :], preferred_element_type=F32)
    b_last = b[c_rows - 1:c_rows, :]
    k_s = (k * jnp.exp(b_last - b)).astype(BF16)
    upd = lax.dot_general(v, k_s, TN_DIMS, preferred_element_type=F32)
    return o, st * jnp.exp(b_last) + upd


def _gla_kernel(qk_ref, v_ref, r_ref, g_ref, gain_ref, o_ref, state_ref, *, n_chunks):
    c_rows = GLA_CHUNK
    nh, dk, dv = GLA_HEADS, GLA_DK, GLA_DV

    @pl.when(pl.program_id(1) == 0)
    def _():
        state_ref[...] = jnp.zeros_like(state_ref)

    row = lax.broadcasted_iota(jnp.int32, (c_rows, c_rows), 0)
    col = lax.broadcasted_iota(jnp.int32, (c_rows, c_rows), 1)
    tril = (row >= col).astype(BF16)

    def chunk(c, carry):
        rows = pl.ds(pl.multiple_of(c * c_rows, c_rows), c_rows)
        for h in range(nh):
            q = qk_ref[rows, h * dk:(h + 1) * dk].astype(F32) * (dk ** -0.5)
            k = qk_ref[rows, (nh + h) * dk:(nh + h + 1) * dk].astype(F32)
            v = v_ref[rows, h * dv:(h + 1) * dv]
            g = g_ref[rows, h * dk:(h + 1) * dk]
            o, st = _gla_chunk_head(q, k, v, g, state_ref[h], tril)
            state_ref[h] = st
            ms = jnp.mean(o * o, axis=-1, keepdims=True)
            on = o * lax.rsqrt(ms + EPS) * gain_ref[...]
            r = r_ref[rows, h * dv:(h + 1) * dv].astype(F32)
            o_ref[rows, h * dv:(h + 1) * dv] = (on * (r * jax.nn.sigmoid(r))).astype(o_ref.dtype)
        return carry

    lax.fori_loop(0, n_chunks, chunk, 0)


def _gla(qk, gv, gr, g, out_gain, batch, seq, ts):
    nh, dk, dv = GLA_HEADS, GLA_DK, GLA_DV
    tok = lambda w: pl.BlockSpec((None, ts, w), lambda b, s: (b, s, 0))
    kern = functools.partial(_gla_kernel, n_chunks=ts // GLA_CHUNK)
    out = pl.pallas_call(
        kern,
        grid=(batch, seq // ts),
        in_specs=[tok(2 * nh * dk), tok(nh * dv), tok(nh * dv), tok(nh * dk),
                  _const_spec((1, dv))],
        out_specs=tok(nh * dv),
        out_shape=jax.ShapeDtypeStruct((batch, seq, nh * dv), BF16),
        scratch_shapes=[pltpu.VMEM((nh, dv, dk), F32)],
        compiler_params=_cparams(("parallel", "arbitrary")),
        name="gla",
    )(qk.reshape(batch, seq, -1), gv.reshape(batch, seq, -1), gr.reshape(batch, seq, -1),
      g.reshape(batch, seq, -1), out_gain)
    return out.reshape(batch * seq, nh * dv)


def _t5_bucket_table():
    w = SWA_WINDOW
    qi = np.arange(w)[:, None]
    kj = np.arange(2 * w)[None, :]
    rel = np.clip(qi + w - kj, 0, None)
    max_exact = REL_BUCKETS // 2
    rel_f = np.maximum(rel, 1).astype(np.float32)
    large = max_exact + (np.log(rel_f / np.float32(max_exact))
                         / np.float32(math.log(REL_MAX_DISTANCE / max_exact))
                         * np.float32(REL_BUCKETS - max_exact)).astype(np.int32)
    large = np.minimum(large, REL_BUCKETS - 1)
    return np.where(rel < max_exact, rel, large).astype(np.int32)


def _window_mask():
    w = SWA_WINDOW
    rel = np.arange(w)[:, None] + w - np.arange(2 * w)[None, :]
    return (rel >= 0) & (rel < w)


def _segment_matrix(width):
    seg = np.arange(width)[:, None] // SWA_HEAD_DIM
    return (seg == np.arange(LANES)[None, :]).astype(np.float32)


def _segment_rsqrt(x, e_ref, et_ref):
    hi, lo = _split_bf16(x * x)
    ss = (jnp.dot(hi, e_ref[...], preferred_element_type=F32)
          + jnp.dot(lo, e_ref[...], preferred_element_type=F32))
    hi, lo = _split_bf16(lax.rsqrt(ss * (1.0 / SWA_HEAD_DIM) + EPS))
    return (jnp.dot(hi, et_ref[...], preferred_element_type=F32)
            + jnp.dot(lo, et_ref[...], preferred_element_type=F32))


def _swa_kernel(q_ref, kp_ref, kc_ref, vp_ref, vc_ref, bias_ref, qs_ref, ks_ref, sink_ref,
                eq_ref, eqt_ref, ek_ref, ekt_ref, o_ref):
    w, hd = SWA_WINDOW, SWA_HEAD_DIM
    group = SWA_Q_HEADS // SWA_KV_HEADS
    n = pl.program_id(1)
    band = 2 * w
    n_pairs = SWA_Q_HEADS // 2
    kj = lax.broadcasted_iota(jnp.int32, (w, 2 * band), 1)
    no_prev = ((kj % band) < w) & (n == 0)
    low_k = lax.broadcasted_iota(jnp.int32, (band, LANES), 1) < hd
    low_q = lax.broadcasted_iota(jnp.int32, (w, LANES), 1) < hd

    q = q_ref[...].astype(F32)
    qn = (q * _segment_rsqrt(q, eq_ref, eqt_ref) * qs_ref[...]).astype(BF16)
    kd = jnp.concatenate([kp_ref[...], kc_ref[...]], axis=0).astype(F32)
    kn = (kd * _segment_rsqrt(kd, ek_ref, ekt_ref) * ks_ref[...]).astype(BF16)
    vd = jnp.concatenate([vp_ref[...], vc_ref[...]], axis=0)
    zero = jnp.zeros((band, LANES), BF16)
    ones_low = jnp.where(low_k, 1.0, 0.0).astype(BF16)
    ones_high = jnp.where(low_k, 0.0, 1.0).astype(BF16)
    k_both, v_both = [], []
    for g in range(SWA_KV_HEADS):
        kg = kn[:, g * LANES:(g + 1) * LANES]
        vg = vd[:, g * LANES:(g + 1) * LANES]
        k_both.append(jnp.concatenate([jnp.where(low_k, kg, zero), jnp.where(low_k, zero, kg)], axis=0))
        v_both.append(jnp.concatenate(
            [jnp.concatenate([jnp.where(low_k, vg, zero), ones_low], axis=1),
             jnp.concatenate([jnp.where(low_k, zero, vg), ones_high], axis=1)],
            axis=0))

    def logits_of(grp):
        qp = qn[:, grp * LANES:(grp + 1) * LANES]
        lg = lax.dot_general(qp, k_both[2 * grp // group], NT_DIMS,
                             preferred_element_type=F32) + bias_ref[grp]
        return jnp.where(no_prev, -jnp.inf, lg)

    def probs_of(grp, lg):
        es, sink_es = [], []
        for half in range(2):
            hq = 2 * grp + half
            sink = sink_ref[0:1, hq:hq + 1]
            lh = lg[:, half * band:(half + 1) * band]
            m = jnp.maximum(jnp.max(lh, axis=-1, keepdims=True), sink)
            es.append(jnp.exp(lh - m).astype(BF16))
            sink_es.append(jnp.exp(sink - m))
        return jnp.concatenate(es, axis=1), jnp.where(low_q, sink_es[0], sink_es[1])

    def out_of(grp, e, sink_e):
        pv = jnp.dot(e, v_both[2 * grp // group], preferred_element_type=F32)
        return pv[:, :LANES] / (pv[:, LANES:] + sink_e)

    lead = 4
    logits, probs = {}, {}
    for step in range(n_pairs + lead):
        if step < n_pairs:
            logits[step] = logits_of(step)
        if 1 <= step <= n_pairs:
            probs[step - 1] = probs_of(step - 1, logits.pop(step - 1))
        if step >= lead:
            grp = step - lead
            o_ref[:, grp * LANES:(grp + 1) * LANES] = out_of(grp, *probs.pop(grp)).astype(o_ref.dtype)


def _swa(sq, skd, svd, bias, q_scale, k_scale, sinks, batch, seq):
    w, hd = SWA_WINDOW, SWA_HEAD_DIM
    hq, hkv = SWA_Q_HEADS, SWA_KV_HEADS
    nq, nkd = hq * hd, hkv * LANES
    eq = _segment_matrix(nq)
    ek = _segment_matrix(nkd)
    prev = lambda b, n: (b, jnp.maximum(n - 1, 0), 0)
    cur = lambda b, n: (b, n, 0)
    out = pl.pallas_call(
        _swa_kernel,
        grid=(batch, seq // w),
        in_specs=[
            pl.BlockSpec((None, w, nq), cur),
            pl.BlockSpec((None, w, nkd), prev),
            pl.BlockSpec((None, w, nkd), cur),
            pl.BlockSpec((None, w, nkd), prev),
            pl.BlockSpec((None, w, nkd), cur),
            _const_spec((hq // 2, w, 4 * w)),
            _const_spec((1, nq)),
            _const_spec((1, nkd)),
            _const_spec((1, hq)),
            _const_spec(eq.shape), _const_spec(eq.T.shape),
            _const_spec(ek.shape), _const_spec(ek.T.shape),
        ],
        out_specs=pl.BlockSpec((None, w, nq), cur),
        out_shape=jax.ShapeDtypeStruct((batch, seq, nq), BF16),
        compiler_params=_cparams(("parallel", "parallel")),
        name="swa",
    )(sq.reshape(batch, seq, nq), *([skd.reshape(batch, seq, nkd)] * 2),
      *([svd.reshape(batch, seq, nkd)] * 2), bias, q_scale, k_scale, sinks,
      jnp.asarray(eq, BF16), jnp.asarray(eq.T, BF16), jnp.asarray(ek, BF16), jnp.asarray(ek.T, BF16))
    return out.reshape(batch * seq, nq)


def _merge_kernel(x_ref, ya_ref, yb_ref, gt_ref, wa_ref, wb_ref, wo_ref, gain_ref,
                  wq_ref, sk_ref, h_ref, hn_ref, sc_ref):
    d = x_ref.shape[1]
    pa = jnp.dot(ya_ref[...], wa_ref[...], preferred_element_type=F32)
    pb = jnp.dot(yb_ref[...], wb_ref[...], preferred_element_type=F32)
    gates = gt_ref[...].astype(F32)
    merged = jax.nn.sigmoid(gates[:, :d]) * pa + jax.nn.sigmoid(gates[:, d:]) * pb
    h = x_ref[...] + jnp.dot(merged.astype(BF16), wo_ref[...], preferred_element_type=F32)
    h_ref[...] = h
    ms = jnp.mean(h * h, axis=-1, keepdims=True)
    hn = (h * lax.rsqrt(ms + EPS) * gain_ref[...]).astype(BF16)
    hn_ref[...] = hn
    q = jnp.dot(hn, wq_ref[...], preferred_element_type=F32)
    hd = PEER_HALF_DIM
    for hp in range(2 * PEER_HEADS):
        sc_ref[hp] = lax.dot_general(sk_ref[hp], q[:, hp * hd:(hp + 1) * hd], NT_DIMS,
                                     preferred_element_type=F32,
                                     precision=lax.Precision.HIGHEST)


def _merge(x2, ya, yb, gates, wa, wb, wo, gain, wq, sk, tm):
    t, d = x2.shape
    nq = wq.shape[1]
    nhp, nk, hd = sk.shape
    tok = lambda w: pl.BlockSpec((tm, w), lambda i: (i, 0))
    return pl.pallas_call(
        _merge_kernel,
        grid=(t // tm,),
        in_specs=[tok(d), tok(ya.shape[1]), tok(yb.shape[1]), tok(2 * d),
                  _const_spec(wa.shape), _const_spec(wb.shape), _const_spec(wo.shape),
                  _const_spec((1, d)), _const_spec((d, nq)), _const_spec((nhp, nk, hd))],
        out_specs=[tok(d), tok(d), pl.BlockSpec((nhp, nk, tm), lambda i: (0, 0, i))],
        out_shape=[jax.ShapeDtypeStruct((t, d), F32),
                   jax.ShapeDtypeStruct((t, d), BF16),
                   jax.ShapeDtypeStruct((nhp, nk, t), F32)],
        compiler_params=_cparams(("parallel",)),
        name="merge",
    )(x2, ya, yb, gates, wa, wb, wo, gain, wq, sk)


def _top_values(s, k, with_rank=False):
    t = s.shape[1]
    row = lax.broadcasted_iota(jnp.int32, (k, t), 0)
    vals = jnp.zeros((k, t), F32)
    rank = jnp.full(s.shape, float(k), F32) if with_rank else None
    cur = s
    for r in range(k):
        m = jnp.max(cur, axis=0, keepdims=True)
        vals = jnp.where(row == r, m, vals)
        hit = cur == m
        if with_rank:
            rank = jnp.where(hit, float(r), rank)
        if r + 1 < k:
            cur = jnp.where(hit, -jnp.inf, cur)
    return vals, rank


def _candidate_sums(t1, t2, k):
    assert k == 16
    sub = 8
    row = lax.broadcasted_iota(jnp.int32, (sub, t1.shape[1]), 0)
    pieces = [t1[0:1, :] + t2]
    for a in range(1, sub):
        pieces.append(jnp.where(row < k // (a + 1), t1[a:a + 1, :] + t2[0:sub, :], -jnp.inf))
    pieces.append(t1[sub:k, :] + t2[0:1, :])
    return jnp.concatenate(pieces, axis=0)


def _peer_select_kernel(sc_ref, rank_ref, e2_ref, n_ref, c_ref):
    kk = PEER_TOPK
    for h in range(PEER_HEADS):
        s1 = sc_ref[2 * h]
        s2 = sc_ref[2 * h + 1]
        t1, _ = _top_values(s1, kk)
        t2, rank2 = _top_values(s2, kk, with_rank=True)
        cand = _candidate_sums(t1, t2, kk)
        tau = _top_values(cand, kk)[0][kk - 1:kk, :]
        m1 = t1[0:1, :]
        m2 = t2[0:1, :]
        z = jnp.sum(jnp.where(cand >= tau, jnp.exp(cand - (m1 + m2)), 0.0),
                    axis=0, keepdims=True)
        count = jnp.zeros(s1.shape, F32)
        for b in range(kk):
            count = count + jnp.where(s1 + t2[b:b + 1, :] >= tau, 1.0, 0.0)
        rank_ref[h] = pltpu.bitcast(rank2.astype(BF16), jnp.uint32)
        e2_ref[h] = pltpu.bitcast(jnp.exp(s2 - m2).astype(BF16), jnp.uint32)
        n_ref[h] = count
        c_ref[h] = jnp.exp(s1 - m1) / z


def _peer_select(scores_t, tn):
    nhp, nk, t = scores_t.shape
    nh = nhp // 2
    out_spec = pl.BlockSpec((nh, nk, tn), lambda i: (0, 0, i))
    packed_spec = pl.BlockSpec((nh, nk // 2, tn), lambda i: (0, 0, i))
    return pl.pallas_call(
        _peer_select_kernel,
        grid=(t // tn,),
        in_specs=[pl.BlockSpec((nhp, nk, tn), lambda i: (0, 0, i))],
        out_specs=[packed_spec, packed_spec, out_spec, out_spec],
        out_shape=[jax.ShapeDtypeStruct((nh, nk // 2, t), jnp.uint32),
                   jax.ShapeDtypeStruct((nh, nk // 2, t), jnp.uint32),
                   jax.ShapeDtypeStruct((nh, nk, t), F32),
                   jax.ShapeDtypeStruct((nh, nk, t), F32)],
        compiler_params=_cparams(("parallel",)),
        name="peer_select",
    )(scores_t)


def _gated_activation(rank_ref, e2_ref, n_ref, c_ref, act_ref, w_ref, row0, n_rows):
    nk = PEER_N_KEYS
    half = nk // 2
    pack = 16
    tm = act_ref.shape[1]
    for tc in range(tm // LANES):
        lanes = slice(tc * LANES, (tc + 1) * LANES)
        for hf in range(2):
            words = slice(hf * half // 2, (hf + 1) * half // 2)
            gates = [None] * n_rows
            for h in range(PEER_HEADS):
                rk = pltpu.bitcast(rank_ref[h, words, lanes], BF16)
                ev = pltpu.bitcast(e2_ref[h, words, lanes], BF16)
                for ii in range(n_rows):
                    i1 = row0 + ii
                    cnt = jnp.broadcast_to(n_ref[h, i1:i1 + 1, lanes], (pack, LANES)).astype(BF16)
                    cf = jnp.broadcast_to(c_ref[h, i1:i1 + 1, lanes], (pack, LANES)).astype(BF16)
                    cnt = jnp.concatenate([cnt] * (half // pack), axis=0)
                    cf = jnp.concatenate([cf] * (half // pack), axis=0)
                    term = jnp.where(rk < cnt, ev * cf, jnp.zeros_like(ev))
                    gates[ii] = term if gates[ii] is None else gates[ii] + term
            for ii in range(n_rows):
                rows = slice(ii * nk + hf * half, ii * nk + (hf + 1) * half)
                a = act_ref[rows, lanes]
                gelu = 0.5 * a * (1.0 + lax.erf(a * (2.0 ** -0.5)))
                w_ref[rows, lanes] = gates[ii] * gelu.astype(BF16)


def _peer_dense_kernel(hn_ref, h_ref, dn_ref, upt_ref, rank_ref, e2_ref, n_ref, c_ref,
                       o_ref, act_ref, w_ref, acc_ref):
    e = pl.program_id(1)
    n_split, ebs, _ = act_ref.shape
    rows = ebs // PEER_N_KEYS

    @pl.when(e == 0)
    def _():
        acc_ref[...] = jnp.zeros_like(acc_ref)

    hn = hn_ref[...]
    for s in range(n_split):
        act_ref[s] = lax.dot_general(dn_ref[s * ebs:(s + 1) * ebs, :], hn, NT_DIMS,
                                     preferred_element_type=F32)
    for s in range(n_split):
        _gated_activation(rank_ref, e2_ref, n_ref, c_ref, act_ref.at[s], w_ref.at[s],
                          s * rows, rows)
    for s in range(n_split):
        acc_ref[...] += jnp.dot(upt_ref[:, s * ebs:(s + 1) * ebs], w_ref[s],
                                preferred_element_type=F32)

    @pl.when(e == pl.num_programs(1) - 1)
    def _():
        o_ref[...] = h_ref[...] + acc_ref[...].T


def _peer_dense(hn, h, dn, upt, rank2, e2, count, coef, tm, eb, n_split):
    t, d = hn.shape
    ne = dn.shape[0]
    nk = PEER_N_KEYS
    nh = PEER_HEADS
    rows = eb // nk
    ebs = eb // n_split
    return pl.pallas_call(
        _peer_dense_kernel,
        grid=(t // tm, ne // eb),
        in_specs=[
            pl.BlockSpec((tm, d), lambda i, e: (i, 0)),
            pl.BlockSpec((tm, d), lambda i, e: (i, 0)),
            pl.BlockSpec((eb, d), lambda i, e: (e, 0)),
            pl.BlockSpec((d, eb), lambda i, e: (0, e)),
            pl.BlockSpec((nh, nk // 2, tm), lambda i, e: (0, 0, i)),
            pl.BlockSpec((nh, nk // 2, tm), lambda i, e: (0, 0, i)),
            pl.BlockSpec((nh, rows, tm), lambda i, e: (0, e, i)),
            pl.BlockSpec((nh, rows, tm), lambda i, e: (0, e, i)),
        ],
        out_specs=pl.BlockSpec((tm, d), lambda i, e: (i, 0)),
        out_shape=jax.ShapeDtypeStruct((t, d), F32),
        scratch_shapes=[pltpu.VMEM((n_split, ebs, tm), F32),
                        pltpu.VMEM((n_split, ebs, tm), BF16),
                        pltpu.VMEM((d, tm), F32)],
        compiler_params=_cparams(("parallel", "arbitrary")),
        name="peer_dense",
    )(hn, h, dn, upt, rank2, e2, count, coef)


def _duplicate_heads(w, n_heads):
    d = w.shape[0]
    w3 = w.reshape(d, n_heads, -1)
    return jnp.concatenate([w3, w3], axis=2).reshape(d, -1)


def _layer(h2, batch, seq, p):
    t, d = h2.shape
    hk, hv = GLA_HEADS * GLA_DK, GLA_HEADS * GLA_DV
    nq, nkv = SWA_Q_HEADS * SWA_HEAD_DIM, SWA_KV_HEADS * SWA_HEAD_DIM
    rank = GLA_GATE_RANK
    tm = min(512, t)
    w_in = p["w_in"]
    bounds = np.cumsum([0, 2 * hk, hv, hv, rank, nq, nkv, nkv, 2 * d])
    assert bounds[-1] == w_in.shape[1]
    w_qk, w_v, w_r, w_ga, w_sq, w_sk, w_sv, w_gt = (
        w_in[:, a:b] for a, b in zip(bounds[:-1], bounds[1:]))
    w_cat = jnp.concatenate(
        [w_qk, w_v, w_r, w_sq, _duplicate_heads(w_sk, SWA_KV_HEADS),
         _duplicate_heads(w_sv, SWA_KV_HEADS), w_gt], axis=1).astype(BF16)
    widths = (2 * hk, hv, hv, nq, 2 * nkv, 2 * nkv, 2 * d)
    w_ga = jnp.pad(w_ga, ((0, 0), (0, LANES - rank))).astype(BF16)
    w_up = jnp.pad(p["w_gate_up"], ((0, LANES - rank), (0, 0)))
    qk, gv, gr, sq, skd, svd, gates, g = _in_proj(
        h2, p["norm_mix_gain"].reshape(1, d), w_cat, w_ga, w_up,
        p["b_gate_up"].reshape(1, hk), widths, tm)

    y_gla = _gla(qk, gv, gr, g, p["gla_out_gain"].reshape(1, GLA_DV), batch, seq, min(512, seq))

    bias = jnp.transpose(p["rel_bias"][_t5_bucket_table()], (2, 0, 1)).astype(F32)
    bias = jnp.where(_window_mask()[None], bias, -jnp.inf)
    bias = bias.reshape(SWA_Q_HEADS // 2, 2, SWA_WINDOW, 2 * SWA_WINDOW)
    bias = jnp.transpose(bias, (0, 2, 1, 3)).reshape(SWA_Q_HEADS // 2, SWA_WINDOW, 4 * SWA_WINDOW)
    q_scale = jnp.tile(p["swa_q_gain"], SWA_Q_HEADS).reshape(1, nq) * (SWA_HEAD_DIM ** -0.5)
    k_scale = jnp.tile(p["swa_k_gain"], 2 * SWA_KV_HEADS).reshape(1, 2 * nkv)
    y_swa = _swa(sq, skd, svd, bias, q_scale, k_scale,
                 p["swa_sinks"].reshape(1, SWA_Q_HEADS), batch, seq)

    sk = p["peer_sub_keys"].reshape(2 * PEER_HEADS, PEER_N_KEYS, PEER_HALF_DIM)
    h_mid, hn, scores_t = _merge(
        h2, y_gla, y_swa, gates, p["w_branch_gla"].astype(BF16),
        p["w_branch_swa"].astype(BF16), p["w_out"].astype(BF16),
        p["norm_ffn_gain"].reshape(1, d), p["peer_w_query"].astype(BF16), sk, min(256, t))

    rank2, e2, count, coef = _peer_select(scores_t, min(256, t))
    dn = p["peer_expert_down"].astype(BF16)
    upt = p["peer_expert_up"].astype(BF16).T
    return _peer_dense(hn, h_mid, dn, upt, rank2, e2, count, coef, tm, 2048, 2)


def kernel(x, norm_mix_gain, w_in, w_gate_up, b_gate_up, gla_out_gain, swa_q_gain,
           swa_k_gain, swa_sinks, rel_bias, w_branch_gla, w_branch_swa, w_out,
           norm_ffn_gain, peer_w_query, peer_sub_keys, peer_expert_down, peer_expert_up):
    batch, seq, d = x.shape
    h2 = x.reshape(batch * seq, d)
    for layer in range(w_in.shape[0]):
        p = dict(norm_mix_gain=norm_mix_gain[layer], w_in=w_in[layer],
                 w_gate_up=w_gate_up[layer], b_gate_up=b_gate_up[layer],
                 gla_out_gain=gla_out_gain[layer], swa_q_gain=swa_q_gain[layer],
                 swa_k_gain=swa_k_gain[layer], swa_sinks=swa_sinks[layer],
                 rel_bias=rel_bias, w_branch_gla=w_branch_gla[layer],
                 w_branch_swa=w_branch_swa[layer], w_out=w_out[layer],
                 norm_ffn_gain=norm_ffn_gain[layer], peer_w_query=peer_w_query[layer],
                 peer_sub_keys=peer_sub_keys[layer], peer_expert_down=peer_expert_down[layer],
                 peer_expert_up=peer_expert_up[layer])
        h2 = _layer(h2, batch, seq, p)
    return h2.reshape(batch, seq, d)
```

```python
import functools
import math

import numpy as np
import jax
import jax.numpy as jnp
from jax import lax
from jax.experimental import pallas as pl
from jax.experimental.pallas import tpu as pltpu

F32 = jnp.float32
BF16 = jnp.bfloat16

EPS = 1e-6

GLA_HEADS = 4
GLA_DK = 128
GLA_DV = 256
GLA_GATE_RANK = 16
GLA_GATE_NORMALIZER = 16.0
GLA_CHUNK = 64
GLA_SUB = 16

SWA_Q_HEADS = 16
SWA_KV_HEADS = 4
SWA_HEAD_DIM = 64
SWA_WINDOW = 128
REL_BUCKETS = 32
REL_MAX_DISTANCE = 128

PEER_HEADS = 8
PEER_N_KEYS = 128
PEER_HALF_DIM = 128
PEER_TOPK = 16

LANES = 128
VMEM_LIMIT = 56 * 1024 * 1024

NT_DIMS = (((1,), (1,)), ((), ()))
TN_DIMS = (((0,), (0,)), ((), ()))


def _cparams(semantics):
    return pltpu.CompilerParams(dimension_semantics=semantics,
                                vmem_limit_bytes=VMEM_LIMIT)


def _const_spec(shape):
    nd = len(shape)
    return pl.BlockSpec(shape, lambda *_: (0,) * nd)


def _split_bf16(x):
    hi = x.astype(BF16)
    lo = (x - hi.astype(F32)).astype(BF16)
    return hi, lo


def _in_proj_kernel(x_ref, g_ref, w_ref, wga_ref, wup_ref, bup_ref, *out_refs):
    x = x_ref[...]
    ms = jnp.mean(x * x, axis=-1, keepdims=True)
    xn = (x * lax.rsqrt(ms + EPS) * g_ref[...]).astype(BF16)
    off = 0
    for o_ref in out_refs[:-1]:
        width = o_ref.shape[1]
        o_ref[...] = jnp.dot(xn, w_ref[:, off:off + width],
                             preferred_element_type=F32).astype(o_ref.dtype)
        off += width
    a_low = jnp.dot(xn, wga_ref[...], preferred_element_type=F32)
    pre = jnp.dot(a_low, wup_ref[...], preferred_element_type=F32,
                  precision=lax.Precision.HIGHEST) + bup_ref[...]
    out_refs[-1][...] = jax.nn.log_sigmoid(pre) * (1.0 / GLA_GATE_NORMALIZER)


def _in_proj(x2, gain, w_cat, w_ga, w_up, b_up, widths, tm):
    t, d = x2.shape
    nw = w_cat.shape[1]
    ng = w_up.shape[1]
    out_shape = [jax.ShapeDtypeStruct((t, w), BF16) for w in widths]
    out_shape.append(jax.ShapeDtypeStruct((t, ng), F32))
    out_specs = [pl.BlockSpec((tm, w), lambda i: (i, 0)) for w in widths + (ng,)]
    return pl.pallas_call(
        _in_proj_kernel,
        grid=(t // tm,),
        in_specs=[pl.BlockSpec((tm, d), lambda i: (i, 0)),
                  _const_spec((1, d)),
                  _const_spec((d, nw)),
                  _const_spec(w_ga.shape),
                  _const_spec(w_up.shape),
                  _const_spec((1, ng))],
        out_specs=out_specs,
        out_shape=out_shape,
        compiler_params=_cparams(("parallel",)),
        name="in_proj",
    )(x2, gain, w_cat, w_ga, w_up, b_up)


def _gla_chunk_head(q, k, v, g, st, tril):
    c_rows, sub = GLA_CHUNK, GLA_SUB
    row_s = lax.broadcasted_iota(jnp.int32, (c_rows, sub), 0)
    sub_j = lax.broadcasted_iota(jnp.int32, (sub, sub), 1)
    sub_r = lax.broadcasted_iota(jnp.int32, (sub, 1), 0)
    g_hi, g_lo = _split_bf16(g)
    b = (jnp.dot(tril, g_hi, preferred_element_type=F32)
         + jnp.dot(tril, g_lo, preferred_element_type=F32))
    o = lax.dot_general((q * jnp.exp(b)).astype(BF16), st.astype(BF16), NT_DIMS,
                        preferred_element_type=F32)
    for j in range(c_rows // sub):
        lo, hi = j * sub, (j + 1) * sub
        b_j, k_j, q_j = b[lo:hi, :], k[lo:hi, :], q[lo:hi, :]
        b_end = b[hi - 1:hi, :]
        diag = jnp.zeros((sub, sub), F32)
        for jj in range(sub):
            dec = jnp.exp(jnp.minimum(b_j - b_j[jj:jj + 1, :], 0.0))
            colv = jnp.sum(q_j * k_j[jj:jj + 1, :] * dec, axis=-1, keepdims=True)
            colv = jnp.where(sub_r >= jj, colv, 0.0)
            diag = jnp.where(sub_j == jj, colv, diag)
        pieces = []
        if lo > 0:
            pieces.append(jnp.zeros((lo, sub), F32))
        pieces.append(diag)
        if hi < c_rows:
            pieces.append(jnp.zeros((c_rows - hi, sub), F32))
        s_col = jnp.concatenate(pieces, axis=0)
        if hi < c_rows:
            k_t = (k_j * jnp.exp(b_end - b_j)).astype(BF16)
            q_t = (q * jnp.exp(jnp.minimum(b - b_end, 0.0))).astype(BF16)
            s_off = lax.dot_general(q_t, k_t, NT_DIMS, preferred_element_type=F32)
            s_col = s_col + jnp.where(row_s >= hi, s_off, 0.0)
        o = o + jnp.dot(s_col.astype(BF16), v[lo:hi, :], preferred_element_type=F32)
    b_last = b[c_rows - 1:c_rows, :]
    k_s = (k * jnp.exp(b_last - b)).astype(BF16)
    upd = lax.dot_general(v, k_s, TN_DIMS, preferred_element_type=F32)
    return o, st * jnp.exp(b_last) + upd


def _gla_kernel(qk_ref, v_ref, r_ref, g_ref, gain_ref, o_ref, state_ref, *, n_chunks):
    c_rows = GLA_CHUNK
    nh, dk, dv = GLA_HEADS, GLA_DK, GLA_DV

    @pl.when(pl.program_id(1) == 0)
    def _():
        state_ref[...] = jnp.zeros_like(state_ref)

    row = lax.broadcasted_iota(jnp.int32, (c_rows, c_rows), 0)
    col = lax.broadcasted_iota(jnp.int32, (c_rows, c_rows), 1)
    tril = (row >= col).astype(BF16)

    def chunk(c, carry):
        rows = pl.ds(pl.multiple_of(c * c_rows, c_rows), c_rows)
        for h in range(nh):
            q = qk_ref[rows, h * dk:(h + 1) * dk].astype(F32) * (dk ** -0.5)
            k = qk_ref[rows, (nh + h) * dk:(nh + h + 1) * dk].astype(F32)
            v = v_ref[rows, h * dv:(h + 1) * dv]
            g = g_ref[rows, h * dk:(h + 1) * dk]
            o, st = _gla_chunk_head(q, k, v, g, state_ref[h], tril)
            state_ref[h] = st
            ms = jnp.mean(o * o, axis=-1, keepdims=True)
            on = o * lax.rsqrt(ms + EPS) * gain_ref[...]
            r = r_ref[rows, h * dv:(h + 1) * dv].astype(F32)
            o_ref[rows, h * dv:(h + 1) * dv] = (on * (r * jax.nn.sigmoid(r))).astype(o_ref.dtype)
        return carry

    lax.fori_loop(0, n_chunks, chunk, 0)


def _gla(qk, gv, gr, g, out_gain, batch, seq, ts):
    nh, dk, dv = GLA_HEADS, GLA_DK, GLA_DV
    tok = lambda w: pl.BlockSpec((None, ts, w), lambda b, s: (b, s, 0))
    kern = functools.partial(_gla_kernel, n_chunks=ts // GLA_CHUNK)
    out = pl.pallas_call(
        kern,
        grid=(batch, seq // ts),
        in_specs=[tok(2 * nh * dk), tok(nh * dv), tok(nh * dv), tok(nh * dk),
                  _const_spec((1, dv))],
        out_specs=tok(nh * dv),
        out_shape=jax.ShapeDtypeStruct((batch, seq, nh * dv), BF16),
        scratch_shapes=[pltpu.VMEM((nh, dv, dk), F32)],
        compiler_params=_cparams(("parallel", "arbitrary")),
        name="gla",
    )(qk.reshape(batch, seq, -1), gv.reshape(batch, seq, -1), gr.reshape(batch, seq, -1),
      g.reshape(batch, seq, -1), out_gain)
    return out.reshape(batch * seq, nh * dv)


def _t5_bucket_table():
    w = SWA_WINDOW
    qi = np.arange(w)[:, None]
    kj = np.arange(2 * w)[None, :]
    rel = np.clip(qi + w - kj, 0, None)
    max_exact = REL_BUCKETS // 2
    rel_f = np.maximum(rel, 1).astype(np.float32)
    large = max_exact + (np.log(rel_f / np.float32(max_exact))
                         / np.float32(math.log(REL_MAX_DISTANCE / max_exact))
                         * np.float32(REL_BUCKETS - max_exact)).astype(np.int32)
    large = np.minimum(large, REL_BUCKETS - 1)
    return np.where(rel < max_exact, rel, large).astype(np.int32)


def _window_mask():
    w = SWA_WINDOW
    rel = np.arange(w)[:, None] + w - np.arange(2 * w)[None, :]
    return (rel >= 0) & (rel < w)


def _segment_matrix(width):
    seg = np.arange(width)[:, None] // SWA_HEAD_DIM
    return (seg == np.arange(LANES)[None, :]).astype(np.float32)


def _segment_rsqrt(x, e_ref, et_ref):
    hi, lo = _split_bf16(x * x)
    ss = (jnp.dot(hi, e_ref[...], preferred_element_type=F32)
          + jnp.dot(lo, e_ref[...], preferred_element_type=F32))
    hi, lo = _split_bf16(lax.rsqrt(ss * (1.0 / SWA_HEAD_DIM) + EPS))
    return (jnp.dot(hi, et_ref[...], preferred_element_type=F32)
            + jnp.dot(lo, et_ref[...], preferred_element_type=F32))


def _swa_kernel(q_ref, kp_ref, kc_ref, vp_ref, vc_ref, bias_ref, qs_ref, ks_ref, sink_ref,
                eq_ref, eqt_ref, ek_ref, ekt_ref, o_ref):
    w, hd = SWA_WINDOW, SWA_HEAD_DIM
    group = SWA_Q_HEADS // SWA_KV_HEADS
    n = pl.program_id(1)
    band = 2 * w
    n_pairs = SWA_Q_HEADS // 2
    kj = lax.broadcasted_iota(jnp.int32, (w, 2 * band), 1)
    no_prev = ((kj % band) < w) & (n == 0)
    low_k = lax.broadcasted_iota(jnp.int32, (band, LANES), 1) < hd
    low_q = lax.broadcasted_iota(jnp.int32, (w, LANES), 1) < hd

    q = q_ref[...].astype(F32)
    qn = (q * _segment_rsqrt(q, eq_ref, eqt_ref) * qs_ref[...]).astype(BF16)
    kd = jnp.concatenate([kp_ref[...], kc_ref[...]], axis=0).astype(F32)
    kn = (kd * _segment_rsqrt(kd, ek_ref, ekt_ref) * ks_ref[...]).astype(BF16)
    vd = jnp.concatenate([vp_ref[...], vc_ref[...]], axis=0)
    zero = jnp.zeros((band, LANES), BF16)
    ones_low = jnp.where(low_k, 1.0, 0.0).astype(BF16)
    ones_high = jnp.where(low_k, 0.0, 1.0).astype(BF16)
    k_both, v_both = [], []
    for g in range(SWA_KV_HEADS):
        kg = kn[:, g * LANES:(g + 1) * LANES]
        vg = vd[:, g * LANES:(g + 1) * LANES]
        k_both.append(jnp.concatenate([jnp.where(low_k, kg, zero), jnp.where(low_k, zero, kg)], axis=0))
        v_both.append(jnp.concatenate(
            [jnp.concatenate([jnp.where(low_k, vg, zero), ones_low], axis=1),
             jnp.concatenate([jnp.where(low_k, zero, vg), ones_high], axis=1)],
            axis=0))

    def logits_of(grp):
        qp = qn[:, grp * LANES:(grp + 1) * LANES]
        lg = lax.dot_general(qp, k_both[2 * grp // group], NT_DIMS,
                             preferred_element_type=F32) + bias_ref[grp]
        return jnp.where(no_prev, -jnp.inf, lg)

    def probs_of(grp, lg):
        es, sink_es = [], []
        for half in range(2):
            hq = 2 * grp + half
            sink = sink_ref[0:1, hq:hq + 1]
            lh = lg[:, half * band:(half + 1) * band]
            m = jnp.maximum(jnp.max(lh, axis=-1, keepdims=True), sink)
            es.append(jnp.exp(lh - m).astype(BF16))
            sink_es.append(jnp.exp(sink - m))
        return jnp.concatenate(es, axis=1), jnp.where(low_q, sink_es[0], sink_es[1])

    def out_of(grp, e, sink_e):
        pv = jnp.dot(e, v_both[2 * grp // group], preferred_element_type=F32)
        return pv[:, :LANES] / (pv[:, LANES:] + sink_e)

    lead = 4
    logits, probs = {}, {}
    for step in range(n_pairs + lead):
        if step < n_pairs:
            logits[step] = logits_of(step)
        if 1 <= step <= n_pairs:
            probs[step - 1] = probs_of(step - 1, logits.pop(step - 1))
        if step >= lead:
            grp = step - lead
            o_ref[:, grp * LANES:(grp + 1) * LANES] = out_of(grp, *probs.pop(grp)).astype(o_ref.dtype)


def _swa(sq, skd, svd, bias, q_scale, k_scale, sinks, batch, seq):
    w, hd = SWA_WINDOW, SWA_HEAD_DIM
    hq, hkv = SWA_Q_HEADS, SWA_KV_HEADS
    nq, nkd = hq * hd, hkv * LANES
    eq = _segment_matrix(nq)
    ek = _segment_matrix(nkd)
    prev = lambda b, n: (b, jnp.maximum(n - 1, 0), 0)
    cur = lambda b, n: (b, n, 0)
    out = pl.pallas_call(
        _swa_kernel,
        grid=(batch, seq // w),
        in_specs=[
            pl.BlockSpec((None, w, nq), cur),
            pl.BlockSpec((None, w, nkd), prev),
            pl.BlockSpec((None, w, nkd), cur),
            pl.BlockSpec((None, w, nkd), prev),
            pl.BlockSpec((None, w, nkd), cur),
            _const_spec((hq // 2, w, 4 * w)),
            _const_spec((1, nq)),
            _const_spec((1, nkd)),
            _const_spec((1, hq)),
            _const_spec(eq.shape), _const_spec(eq.T.shape),
            _const_spec(ek.shape), _const_spec(ek.T.shape),
        ],
        out_specs=pl.BlockSpec((None, w, nq), cur),
        out_shape=jax.ShapeDtypeStruct((batch, seq, nq), BF16),
        compiler_params=_cparams(("parallel", "parallel")),
        name="swa",
    )(sq.reshape(batch, seq, nq), *([skd.reshape(batch, seq, nkd)] * 2),
      *([svd.reshape(batch, seq, nkd)] * 2), bias, q_scale, k_scale, sinks,
      jnp.asarray(eq, BF16), jnp.asarray(eq.T, BF16), jnp.asarray(ek, BF16), jnp.asarray(ek.T, BF16))
    return out.reshape(batch * seq, nq)


def _top_values(s, k, with_rank=False):
    t = s.shape[1]
    row = lax.broadcasted_iota(jnp.int32, (k, t), 0)
    vals = jnp.zeros((k, t), F32)
    rank = jnp.full(s.shape, float(k), F32) if with_rank else None
    cur = s
    for r in range(k):
        m = jnp.max(cur, axis=0, keepdims=True)
        vals = jnp.where(row == r, m, vals)
        hit = cur == m
        if with_rank:
            rank = jnp.where(hit, float(r), rank)
        if r + 1 < k:
            cur = jnp.where(hit, -jnp.inf, cur)
    return vals, rank


def _candidate_sums(t1, t2, k):
    assert k == 16
    sub = 8
    row = lax.broadcasted_iota(jnp.int32, (sub, t1.shape[1]), 0)
    pieces = [t1[0:1, :] + t2]
    for a in range(1, sub):
        pieces.append(jnp.where(row < k // (a + 1), t1[a:a + 1, :] + t2[0:sub, :], -jnp.inf))
    pieces.append(t1[sub:k, :] + t2[0:1, :])
    return jnp.concatenate(pieces, axis=0)


def _select_head(s1, s2):
    kk = PEER_TOPK
    t1, _ = _top_values(s1, kk)
    t2, rank2 = _top_values(s2, kk, with_rank=True)
    cand = _candidate_sums(t1, t2, kk)
    tau = _top_values(cand, kk)[0][kk - 1:kk, :]
    m1 = t1[0:1, :]
    m2 = t2[0:1, :]
    z = jnp.sum(jnp.where(cand >= tau, jnp.exp(cand - (m1 + m2)), 0.0), axis=0, keepdims=True)
    count = jnp.zeros(s1.shape, F32)
    for b in range(kk):
        count = count + jnp.where(s1 + t2[b:b + 1, :] >= tau, 1.0, 0.0)
    return (pltpu.bitcast(rank2.astype(BF16), jnp.uint32),
            pltpu.bitcast(jnp.exp(s2 - m2).astype(BF16), jnp.uint32),
            count, jnp.exp(s1 - m1) / z)


def _merge_select_kernel(x_ref, ya_ref, yb_ref, gt_ref, wa_ref, wb_ref, wo_ref, gain_ref,
                         wq_ref, sk_ref, h_ref, hn_ref, rank_ref, e2_ref, n_ref, c_ref, sc_ref):
    d = x_ref.shape[1]

    @pl.when(pl.program_id(0) == 0)
    def _():
        sc_ref[...] = jnp.zeros_like(sc_ref)

    pa = jnp.dot(ya_ref[...], wa_ref[...], preferred_element_type=F32)
    pb = jnp.dot(yb_ref[...], wb_ref[...], preferred_element_type=F32)
    gates = gt_ref[...].astype(F32)
    merged = jax.nn.sigmoid(gates[:, :d]) * pa + jax.nn.sigmoid(gates[:, d:]) * pb
    h = x_ref[...] + jnp.dot(merged.astype(BF16), wo_ref[...], preferred_element_type=F32)
    h_ref[...] = h
    ms = jnp.mean(h * h, axis=-1, keepdims=True)
    hn = (h * lax.rsqrt(ms + EPS) * gain_ref[...]).astype(BF16)
    hn_ref[...] = hn
    q = jnp.dot(hn, wq_ref[...], preferred_element_type=F32)
    hd = PEER_HALF_DIM
    for head in range(PEER_HEADS):
        rank_ref[head], e2_ref[head], n_ref[head], c_ref[head] = _select_head(
            sc_ref[2 * head], sc_ref[2 * head + 1])
        for hp in (2 * head, 2 * head + 1):
            sc_ref[hp] = lax.dot_general(sk_ref[hp], q[:, hp * hd:(hp + 1) * hd], NT_DIMS,
                                         preferred_element_type=F32,
                                         precision=lax.Precision.HIGHEST)


def _merge_select(x2, ya, yb, gates, wa, wb, wo, gain, wq, sk, tm):
    t, d = x2.shape
    nq = wq.shape[1]
    nhp, nk, hd = sk.shape
    nh = nhp // 2
    n_tiles = t // tm
    cur = lambda i: (jnp.minimum(i, n_tiles - 1), 0)
    prev = lambda i: (0, 0, jnp.maximum(i - 1, 0))
    tok = lambda w: pl.BlockSpec((tm, w), cur)
    return pl.pallas_call(
        _merge_select_kernel,
        grid=(n_tiles + 1,),
        in_specs=[tok(d), tok(ya.shape[1]), tok(yb.shape[1]), tok(2 * d),
                  _const_spec(wa.shape), _const_spec(wb.shape), _const_spec(wo.shape),
                  _const_spec((1, d)), _const_spec((d, nq)), _const_spec((nhp, nk, hd))],
        out_specs=[tok(d), tok(d),
                   pl.BlockSpec((nh, nk // 2, tm), prev), pl.BlockSpec((nh, nk // 2, tm), prev),
                   pl.BlockSpec((nh, nk, tm), prev), pl.BlockSpec((nh, nk, tm), prev)],
        out_shape=[jax.ShapeDtypeStruct((t, d), F32),
                   jax.ShapeDtypeStruct((t, d), BF16),
                   jax.ShapeDtypeStruct((nh, nk // 2, t), jnp.uint32),
                   jax.ShapeDtypeStruct((nh, nk // 2, t), jnp.uint32),
                   jax.ShapeDtypeStruct((nh, nk, t), F32),
                   jax.ShapeDtypeStruct((nh, nk, t), F32)],
        scratch_shapes=[pltpu.VMEM((nhp, nk, tm), F32)],
        compiler_params=_cparams(("arbitrary",)),
        name="merge_select",
    )(x2, ya, yb, gates, wa, wb, wo, gain, wq, sk)


def _gated_activation(rank_ref, e2_ref, n_ref, c_ref, act_ref, w_ref, row0, n_rows):
    nk = PEER_N_KEYS
    half = nk // 2
    pack = 16
    tm = act_ref.shape[1]
    for tc in range(tm // LANES):
        lanes = slice(tc * LANES, (tc + 1) * LANES)
        for hf in range(2):
            words = slice(hf * half // 2, (hf + 1) * half // 2)
            gates = [None] * n_rows
            for h in range(PEER_HEADS):
                rk = pltpu.bitcast(rank_ref[h, words, lanes], BF16)
                ev = pltpu.bitcast(e2_ref[h, words, lanes], BF16)
                for ii in range(n_rows):
                    i1 = row0 + ii
                    cnt = jnp.broadcast_to(n_ref[h, i1:i1 + 1, lanes], (pack, LANES)).astype(BF16)
                    cf = jnp.broadcast_to(c_ref[h, i1:i1 + 1, lanes], (pack, LANES)).astype(BF16)
                    cnt = jnp.concatenate([cnt] * (half // pack), axis=0)
                    cf = jnp.concatenate([cf] * (half // pack), axis=0)
                    term = jnp.where(rk < cnt, ev * cf, jnp.zeros_like(ev))
                    gates[ii] = term if gates[ii] is None else gates[ii] + term
            for ii in range(n_rows):
                rows = slice(ii * nk + hf * half, ii * nk + (hf + 1) * half)
                a = act_ref[rows, lanes]
                gelu = 0.5 * a * (1.0 + lax.erf(a * (2.0 ** -0.5)))
                w_ref[rows, lanes] = gates[ii] * gelu.astype(BF16)


def _peer_dense_kernel(hn_ref, h_ref, dn_ref, upt_ref, rank_ref, e2_ref, n_ref, c_ref,
                       o_ref, act_ref, w_ref, acc_ref):
    e = pl.program_id(1)
    n_split, ebs, _ = act_ref.shape
    rows = ebs // PEER_N_KEYS

    @pl.when(e == 0)
    def _():
        acc_ref[...] = jnp.zeros_like(acc_ref)

    hn = hn_ref[...]
    for s in range(n_split):
        act_ref[s] = lax.dot_general(dn_ref[s * ebs:(s + 1) * ebs, :], hn, NT_DIMS,
                                     preferred_element_type=F32)
    for s in range(n_split):
        _gated_activation(rank_ref, e2_ref, n_ref, c_ref, act_ref.at[s], w_ref.at[s],
                          s * rows, rows)
    w = w_ref[...].reshape(n_split * ebs, w_ref.shape[2])
    acc_ref[...] += jnp.dot(upt_ref[...], w, preferred_element_type=F32)

    @pl.when(e == pl.num_programs(1) - 1)
    def _():
        o_ref[...] = h_ref[...] + acc_ref[...].T


def _peer_dense(hn, h, dn, upt, rank2, e2, count, coef, tm, eb, n_split):
    t, d = hn.shape
    ne = dn.shape[0]
    nk = PEER_N_KEYS
    nh = PEER_HEADS
    rows = eb // nk
    ebs = eb // n_split
    return pl.pallas_call(
        _peer_dense_kernel,
        grid=(t // tm, ne // eb),
        in_specs=[
            pl.BlockSpec((tm, d), lambda i, e: (i, 0)),
            pl.BlockSpec((tm, d), lambda i, e: (i, 0)),
            pl.BlockSpec((eb, d), lambda i, e: (e, 0)),
            pl.BlockSpec((d, eb), lambda i, e: (0, e)),
            pl.BlockSpec((nh, nk // 2, tm), lambda i, e: (0, 0, i)),
            pl.BlockSpec((nh, nk // 2, tm), lambda i, e: (0, 0, i)),
            pl.BlockSpec((nh, rows, tm), lambda i, e: (0, e, i)),
            pl.BlockSpec((nh, rows, tm), lambda i, e: (0, e, i)),
        ],
        out_specs=pl.BlockSpec((tm, d), lambda i, e: (i, 0)),
        out_shape=jax.ShapeDtypeStruct((t, d), F32),
        scratch_shapes=[pltpu.VMEM((n_split, ebs, tm), F32),
                        pltpu.VMEM((n_split, ebs, tm), BF16),
                        pltpu.VMEM((d, tm), F32)],
        compiler_params=_cparams(("parallel", "arbitrary")),
        name="peer_dense",
    )(hn, h, dn, upt, rank2, e2, count, coef)


def _duplicate_heads(w, n_heads):
    d = w.shape[0]
    w3 = w.reshape(d, n_heads, -1)
    return jnp.concatenate([w3, w3], axis=2).reshape(d, -1)


def _layer(h2, batch, seq, p):
    t, d = h2.shape
    hk, hv = GLA_HEADS * GLA_DK, GLA_HEADS * GLA_DV
    nq, nkv = SWA_Q_HEADS * SWA_HEAD_DIM, SWA_KV_HEADS * SWA_HEAD_DIM
    rank = GLA_GATE_RANK
    tm = min(512, t)
    w_in = p["w_in"]
    bounds = np.cumsum([0, 2 * hk, hv, hv, rank, nq, nkv, nkv, 2 * d])
    assert bounds[-1] == w_in.shape[1]
    w_qk, w_v, w_r, w_ga, w_sq, w_sk, w_sv, w_gt = (
        w_in[:, a:b] for a, b in zip(bounds[:-1], bounds[1:]))
    w_cat = jnp.concatenate(
        [w_qk, w_v, w_r, w_sq, _duplicate_heads(w_sk, SWA_KV_HEADS),
         _duplicate_heads(w_sv, SWA_KV_HEADS), w_gt], axis=1).astype(BF16)
    widths = (2 * hk, hv, hv, nq, 2 * nkv, 2 * nkv, 2 * d)
    w_ga = jnp.pad(w_ga, ((0, 0), (0, LANES - rank))).astype(BF16)
    w_up = jnp.pad(p["w_gate_up"], ((0, LANES - rank), (0, 0)))
    qk, gv, gr, sq, skd, svd, gates, g = _in_proj(
        h2, p["norm_mix_gain"].reshape(1, d), w_cat, w_ga, w_up,
        p["b_gate_up"].reshape(1, hk), widths, tm)

    y_gla = _gla(qk, gv, gr, g, p["gla_out_gain"].reshape(1, GLA_DV), batch, seq, min(512, seq))

    one_hot = (np.arange(REL_BUCKETS)[:, None] == _t5_bucket_table().reshape(1, -1)).astype(np.float32)
    bias = jnp.dot(p["rel_bias"].astype(F32).T, one_hot, precision=lax.Precision.HIGHEST)
    bias = bias.reshape(SWA_Q_HEADS, SWA_WINDOW, 2 * SWA_WINDOW)
    bias = jnp.where(_window_mask()[None], bias, -jnp.inf)
    bias = bias.reshape(SWA_Q_HEADS // 2, 2, SWA_WINDOW, 2 * SWA_WINDOW)
    bias = jnp.transpose(bias, (0, 2, 1, 3)).reshape(SWA_Q_HEADS // 2, SWA_WINDOW, 4 * SWA_WINDOW)
    q_scale = jnp.tile(p["swa_q_gain"], SWA_Q_HEADS).reshape(1, nq) * (SWA_HEAD_DIM ** -0.5)
    k_scale = jnp.tile(p["swa_k_gain"], 2 * SWA_KV_HEADS).reshape(1, 2 * nkv)
    y_swa = _swa(sq, skd, svd, bias, q_scale, k_scale,
                 p["swa_sinks"].reshape(1, SWA_Q_HEADS), batch, seq)

    sk = p["peer_sub_keys"].reshape(2 * PEER_HEADS, PEER_N_KEYS, PEER_HALF_DIM)
    h_mid, hn, rank2, e2, count, coef = _merge_select(
        h2, y_gla, y_swa, gates, p["w_branch_gla"].astype(BF16),
        p["w_branch_swa"].astype(BF16), p["w_out"].astype(BF16),
        p["norm_ffn_gain"].reshape(1, d), p["peer_w_query"].astype(BF16), sk, min(256, t))
    dn = p["peer_expert_down"].astype(BF16)
    upt = p["peer_expert_up"].astype(BF16).T
    return _peer_dense(hn, h_mid, dn, upt, rank2, e2, count, coef, tm, 2048, 2)


def kernel(x, norm_mix_gain, w_in, w_gate_up, b_gate_up, gla_out_gain, swa_q_gain,
           swa_k_gain, swa_sinks, rel_bias, w_branch_gla, w_branch_swa, w_out,
           norm_ffn_gain, peer_w_query, peer_sub_keys, peer_expert_down, peer_expert_up):
    batch, seq, d = x.shape
    h2 = x.reshape(batch * seq, d)
    for layer in range(w_in.shape[0]):
        p = dict(norm_mix_gain=norm_mix_gain[layer], w_in=w_in[layer],
                 w_gate_up=w_gate_up[layer], b_gate_up=b_gate_up[layer],
                 gla_out_gain=gla_out_gain[layer], swa_q_gain=swa_q_gain[layer],
                 swa_k_gain=swa_k_gain[layer], swa_sinks=swa_sinks[layer],
                 rel_bias=rel_bias, w_branch_gla=w_branch_gla[layer],
                 w_branch_swa=w_branch_swa[layer], w_out=w_out[layer],
                 norm_ffn_gain=norm_ffn_gain[layer], peer_w_query=peer_w_query[layer],
                 peer_sub_keys=peer_sub_keys[layer], peer_expert_down=peer_expert_down[layer],
                 peer_expert_up=peer_expert_up[layer])
        h2 = _layer(h2, batch, seq, p)
    return h2.reshape(batch, seq, d)
```

```python
import functools
import math

import numpy as np
import jax
import jax.numpy as jnp
from jax import lax
from jax.experimental import pallas as pl
from jax.experimental.pallas import tpu as pltpu

F32 = jnp.float32
BF16 = jnp.bfloat16

EPS = 1e-6

GLA_HEADS = 4
GLA_DK = 128
GLA_DV = 256
GLA_GATE_RANK = 16
GLA_GATE_NORMALIZER = 16.0
GLA_CHUNK = 64
GLA_SUB = 16

SWA_Q_HEADS = 16
SWA_KV_HEADS = 4
SWA_HEAD_DIM = 64
SWA_WINDOW = 128
REL_BUCKETS = 32
REL_MAX_DISTANCE = 128

PEER_HEADS = 8
PEER_N_KEYS = 128
PEER_HALF_DIM = 128
PEER_TOPK = 16

LANES = 128
VMEM_LIMIT = 56 * 1024 * 1024

NT_DIMS = (((1,), (1,)), ((), ()))
TN_DIMS = (((0,), (0,)), ((), ()))


def _cparams(semantics):
    return pltpu.CompilerParams(dimension_semantics=semantics,
                                vmem_limit_bytes=VMEM_LIMIT)


def _const_spec(shape):
    nd = len(shape)
    return pl.BlockSpec(shape, lambda *_: (0,) * nd)


def _split_bf16(x):
    hi = x.astype(BF16)
    lo = (x - hi.astype(F32)).astype(BF16)
    return hi, lo


def _pack_rows_kernel(x_ref, o_ref):
    o_ref[...] = pltpu.bitcast(x_ref[...].astype(BF16), jnp.uint32)


def _pack_rows(x, tr):
    r, c = x.shape
    return pl.pallas_call(
        _pack_rows_kernel,
        grid=(r // tr,),
        in_specs=[pl.BlockSpec((tr, c), lambda i: (i, 0))],
        out_specs=pl.BlockSpec((tr // 2, c), lambda i: (i, 0)),
        out_shape=jax.ShapeDtypeStruct((r // 2, c), jnp.uint32),
        compiler_params=_cparams(("parallel",)),
        name="pack_rows",
    )(x)


def _in_proj_kernel(x_ref, g_ref, w_ref, wga_ref, wup_ref, bup_ref, *out_refs):
    x = x_ref[...]
    ms = jnp.mean(x * x, axis=-1, keepdims=True)
    xn = (x * lax.rsqrt(ms + EPS) * g_ref[...]).astype(BF16)
    off = 0
    for o_ref in out_refs[:-1]:
        width = o_ref.shape[1]
        o_ref[...] = jnp.dot(xn, w_ref[:, off:off + width],
                             preferred_element_type=F32).astype(o_ref.dtype)
        off += width
    a_low = jnp.dot(xn, wga_ref[...], preferred_element_type=F32)
    pre = jnp.dot(a_low, wup_ref[...], preferred_element_type=F32,
                  precision=lax.Precision.HIGHEST) + bup_ref[...]
    out_refs[-1][...] = jax.nn.log_sigmoid(pre) * (1.0 / GLA_GATE_NORMALIZER)


def _in_proj(x2, gain, w_cat, w_ga, w_up, b_up, widths, tm):
    t, d = x2.shape
    nw = w_cat.shape[1]
    ng = w_up.shape[1]
    out_shape = [jax.ShapeDtypeStruct((t, w), BF16) for w in widths]
    out_shape.append(jax.ShapeDtypeStruct((t, ng), F32))
    out_specs = [pl.BlockSpec((tm, w), lambda i: (i, 0)) for w in widths + (ng,)]
    return pl.pallas_call(
        _in_proj_kernel,
        grid=(t // tm,),
        in_specs=[pl.BlockSpec((tm, d), lambda i: (i, 0)),
                  _const_spec((1, d)),
                  _const_spec((d, nw)),
                  _const_spec(w_ga.shape),
                  _const_spec(w_up.shape),
                  _const_spec((1, ng))],
        out_specs=out_specs,
        out_shape=out_shape,
        compiler_params=_cparams(("parallel",)),
        name="in_proj",
    )(x2, gain, w_cat, w_ga, w_up, b_up)


def _gla_chunk_head(q, k, v, g, st, tril):
    c_rows, sub = GLA_CHUNK, GLA_SUB
    row_s = lax.broadcasted_iota(jnp.int32, (c_rows, sub), 0)
    sub_j = lax.broadcasted_iota(jnp.int32, (sub, sub), 1)
    sub_r = lax.broadcasted_iota(jnp.int32, (sub, 1), 0)
    g_hi, g_lo = _split_bf16(g)
    b = (jnp.dot(tril, g_hi, preferred_element_type=F32)
         + jnp.dot(tril, g_lo, preferred_element_type=F32))
    o = lax.dot_general((q * jnp.exp(b)).astype(BF16), st.astype(BF16), NT_DIMS,
                        preferred_element_type=F32)
    for j in range(c_rows // sub):
        lo, hi = j * sub, (j + 1) * sub
        b_j, k_j, q_j = b[lo:hi, :], k[lo:hi, :], q[lo:hi, :]
        b_end = b[hi - 1:hi, :]
        diag = jnp.zeros((sub, sub), F32)
        for jj in range(sub):
            dec = jnp.exp(jnp.minimum(b_j - b_j[jj:jj + 1, :], 0.0))
            colv = jnp.sum(q_j * k_j[jj:jj + 1, :] * dec, axis=-1, keepdims=True)
            colv = jnp.where(sub_r >= jj, colv, 0.0)
            diag = jnp.where(sub_j == jj, colv, diag)
        pieces = []
        if lo > 0:
            pieces.append(jnp.zeros((lo, sub), F32))
        pieces.append(diag)
        if hi < c_rows:
            pieces.append(jnp.zeros((c_rows - hi, sub), F32))
        s_col = jnp.concatenate(pieces, axis=0)
        if hi < c_rows:
            k_t = (k_j * jnp.exp(b_end - b_j)).astype(BF16)
            q_t = (q * jnp.exp(jnp.minimum(b - b_end, 0.0))).astype(BF16)
            s_off = lax.dot_general(q_t, k_t, NT_DIMS, preferred_element_type=F32)
            s_col = s_col + jnp.where(row_s >= hi, s_off, 0.0)
        o = o + jnp.dot(s_col.astype(BF16), v[lo:hi, :], preferred_element_type=F32)
    b_last = b[c_rows - 1:c_rows, :]
    k_s = (k * jnp.exp(b_last - b)).astype(BF16)
    upd = lax.dot_general(v, k_s, TN_DIMS, preferred_element_type=F32)
    return o, st * jnp.exp(b_last) + upd


def _gla_kernel(qk_ref, v_ref, r_ref, g_ref, gain_ref, o_ref, state_ref, *, n_chunks):
    c_rows = GLA_CHUNK
    nh, dk, dv = GLA_HEADS, GLA_DK, GLA_DV

    @pl.when(pl.program_id(1) == 0)
    def _():
        state_ref[...] = jnp.zeros_like(state_ref)

    row = lax.broadcasted_iota(jnp.int32, (c_rows, c_rows), 0)
    col = lax.broadcasted_iota(jnp.int32, (c_rows, c_rows), 1)
    tril = (row >= col).astype(BF16)

    def chunk(c, carry):
        rows = pl.ds(pl.multiple_of(c * c_rows, c_rows), c_rows)
        for h in range(nh):
            q = qk_ref[rows, h * dk:(h + 1) * dk].astype(F32) * (dk ** -0.5)
            k = qk_ref[rows, (nh + h) * dk:(nh + h + 1) * dk].astype(F32)
            v = v_ref[rows, h * dv:(h + 1) * dv]
            g = g_ref[rows, h * dk:(h + 1) * dk]
            o, st = _gla_chunk_head(q, k, v, g, state_ref[h], tril)
            state_ref[h] = st
            ms = jnp.mean(o * o, axis=-1, keepdims=True)
            on = o * lax.rsqrt(ms + EPS) * gain_ref[...]
            r = r_ref[rows, h * dv:(h + 1) * dv].astype(F32)
            o_ref[rows, h * dv:(h + 1) * dv] = (on * (r * jax.nn.sigmoid(r))).astype(o_ref.dtype)
        return carry

    lax.fori_loop(0, n_chunks, chunk, 0)


def _gla(qk, gv, gr, g, out_gain, batch, seq, ts):
    nh, dk, dv = GLA_HEADS, GLA_DK, GLA_DV
    tok = lambda w: pl.BlockSpec((None, ts, w), lambda b, s: (b, s, 0))
    kern = functools.partial(_gla_kernel, n_chunks=ts // GLA_CHUNK)
    out = pl.pallas_call(
        kern,
        grid=(batch, seq // ts),
        in_specs=[tok(2 * nh * dk), tok(nh * dv), tok(nh * dv), tok(nh * dk),
                  _const_spec((1, dv))],
        out_specs=tok(nh * dv),
        out_shape=jax.ShapeDtypeStruct((batch, seq, nh * dv), BF16),
        scratch_shapes=[pltpu.VMEM((nh, dv, dk), F32)],
        compiler_params=_cparams(("parallel", "arbitrary")),
        name="gla",
    )(qk.reshape(batch, seq, -1), gv.reshape(batch, seq, -1), gr.reshape(batch, seq, -1),
      g.reshape(batch, seq, -1), out_gain)
    return out.reshape(batch * seq, nh * dv)


def _t5_bucket_table():
    w = SWA_WINDOW
    qi = np.arange(w)[:, None]
    kj = np.arange(2 * w)[None, :]
    rel = np.clip(qi + w - kj, 0, None)
    max_exact = REL_BUCKETS // 2
    rel_f = np.maximum(rel, 1).astype(np.float32)
    large = max_exact + (np.log(rel_f / np.float32(max_exact))
                         / np.float32(math.log(REL_MAX_DISTANCE / max_exact))
                         * np.float32(REL_BUCKETS - max_exact)).astype(np.int32)
    large = np.minimum(large, REL_BUCKETS - 1)
    return np.where(rel < max_exact, rel, large).astype(np.int32)


def _window_mask():
    w = SWA_WINDOW
    rel = np.arange(w)[:, None] + w - np.arange(2 * w)[None, :]
    return (rel >= 0) & (rel < w)


def _segment_matrix(width):
    seg = np.arange(width)[:, None] // SWA_HEAD_DIM
    return (seg == np.arange(LANES)[None, :]).astype(np.float32)


def _segment_rsqrt(x, e_ref, et_ref):
    hi, lo = _split_bf16(x * x)
    ss = (jnp.dot(hi, e_ref[...], preferred_element_type=F32)
          + jnp.dot(lo, e_ref[...], preferred_element_type=F32))
    hi, lo = _split_bf16(lax.rsqrt(ss * (1.0 / SWA_HEAD_DIM) + EPS))
    return (jnp.dot(hi, et_ref[...], preferred_element_type=F32)
            + jnp.dot(lo, et_ref[...], preferred_element_type=F32))


def _swa_kernel(q_ref, kp_ref, kc_ref, vp_ref, vc_ref, bias_ref, qs_ref, ks_ref, sink_ref,
                eq_ref, eqt_ref, ek_ref, ekt_ref, o_ref):
    w, hd = SWA_WINDOW, SWA_HEAD_DIM
    group = SWA_Q_HEADS // SWA_KV_HEADS
    n = pl.program_id(1)
    band = 2 * w
    n_pairs = SWA_Q_HEADS // 2
    kj = lax.broadcasted_iota(jnp.int32, (w, 2 * band), 1)
    no_prev = ((kj % band) < w) & (n == 0)
    low_k = lax.broadcasted_iota(jnp.int32, (band, LANES), 1) < hd
    low_q = lax.broadcasted_iota(jnp.int32, (w, LANES), 1) < hd

    q = q_ref[...].astype(F32)
    qn = (q * _segment_rsqrt(q, eq_ref, eqt_ref) * qs_ref[...]).astype(BF16)
    kd = jnp.concatenate([kp_ref[...], kc_ref[...]], axis=0).astype(F32)
    kn = (kd * _segment_rsqrt(kd, ek_ref, ekt_ref) * ks_ref[...]).astype(BF16)
    vd = jnp.concatenate([vp_ref[...], vc_ref[...]], axis=0)
    zero = jnp.zeros((band, LANES), BF16)
    ones_low = jnp.where(low_k, 1.0, 0.0).astype(BF16)
    ones_high = jnp.where(low_k, 0.0, 1.0).astype(BF16)
    k_both, v_both = [], []
    for g in range(SWA_KV_HEADS):
        kg = kn[:, g * LANES:(g + 1) * LANES]
        vg = vd[:, g * LANES:(g + 1) * LANES]
        k_both.append(jnp.concatenate([jnp.where(low_k, kg, zero), jnp.where(low_k, zero, kg)], axis=0))
        v_both.append(jnp.concatenate(
            [jnp.concatenate([jnp.where(low_k, vg, zero), ones_low], axis=1),
             jnp.concatenate([jnp.where(low_k, zero, vg), ones_high], axis=1)],
            axis=0))

    def logits_of(grp):
        qp = qn[:, grp * LANES:(grp + 1) * LANES]
        lg = lax.dot_general(qp, k_both[2 * grp // group], NT_DIMS,
                             preferred_element_type=F32) + bias_ref[grp]
        return jnp.where(no_prev, -jnp.inf, lg)

    def probs_of(grp, lg):
        es, sink_es = [], []
        for half in range(2):
            hq = 2 * grp + half
            sink = sink_ref[0:1, hq:hq + 1]
            lh = lg[:, half * band:(half + 1) * band]
            m = jnp.maximum(jnp.max(lh, axis=-1, keepdims=True), sink)
            es.append(jnp.exp(lh - m).astype(BF16))
            sink_es.append(jnp.exp(sink - m))
        return jnp.concatenate(es, axis=1), jnp.where(low_q, sink_es[0], sink_es[1])

    def out_of(grp, e, sink_e):
        pv = jnp.dot(e, v_both[2 * grp // group], preferred_element_type=F32)
        return pv[:, :LANES] / (pv[:, LANES:] + sink_e)

    lead = 4
    logits, probs = {}, {}
    for step in range(n_pairs + lead):
        if step < n_pairs:
            logits[step] = logits_of(step)
        if 1 <= step <= n_pairs:
            probs[step - 1] = probs_of(step - 1, logits.pop(step - 1))
        if step >= lead:
            grp = step - lead
            o_ref[:, grp * LANES:(grp + 1) * LANES] = out_of(grp, *probs.pop(grp)).astype(o_ref.dtype)


def _swa(sq, skd, svd, bias, q_scale, k_scale, sinks, batch, seq):
    w, hd = SWA_WINDOW, SWA_HEAD_DIM
    hq, hkv = SWA_Q_HEADS, SWA_KV_HEADS
    nq, nkd = hq * hd, hkv * LANES
    eq = _segment_matrix(nq)
    ek = _segment_matrix(nkd)
    prev = lambda b, n: (b, jnp.maximum(n - 1, 0), 0)
    cur = lambda b, n: (b, n, 0)
    out = pl.pallas_call(
        _swa_kernel,
        grid=(batch, seq // w),
        in_specs=[
            pl.BlockSpec((None, w, nq), cur),
            pl.BlockSpec((None, w, nkd), prev),
            pl.BlockSpec((None, w, nkd), cur),
            pl.BlockSpec((None, w, nkd), prev),
            pl.BlockSpec((None, w, nkd), cur),
            _const_spec((hq // 2, w, 4 * w)),
            _const_spec((1, nq)),
            _const_spec((1, nkd)),
            _const_spec((1, hq)),
            _const_spec(eq.shape), _const_spec(eq.T.shape),
            _const_spec(ek.shape), _const_spec(ek.T.shape),
        ],
        out_specs=pl.BlockSpec((None, w, nq), cur),
        out_shape=jax.ShapeDtypeStruct((batch, seq, nq), BF16),
        compiler_params=_cparams(("parallel", "parallel")),
        name="swa",
    )(sq.reshape(batch, seq, nq), *([skd.reshape(batch, seq, nkd)] * 2),
      *([svd.reshape(batch, seq, nkd)] * 2), bias, q_scale, k_scale, sinks,
      jnp.asarray(eq, BF16), jnp.asarray(eq.T, BF16), jnp.asarray(ek, BF16), jnp.asarray(ek.T, BF16))
    return out.reshape(batch * seq, nq)


def _top_values(s, k, with_rank=False):
    t = s.shape[1]
    row = lax.broadcasted_iota(jnp.int32, (k, t), 0)
    vals = jnp.zeros((k, t), F32)
    rank = jnp.full(s.shape, float(k), F32) if with_rank else None
    cur = s
    for r in range(k):
        m = jnp.max(cur, axis=0, keepdims=True)
        vals = jnp.where(row == r, m, vals)
        hit = cur == m
        if with_rank:
            rank = jnp.where(hit, float(r), rank)
        if r + 1 < k:
            cur = jnp.where(hit, -jnp.inf, cur)
    return vals, rank


def _candidate_sums(t1, t2, k):
    assert k == 16
    sub = 8
    row = lax.broadcasted_iota(jnp.int32, (sub, t1.shape[1]), 0)
    pieces = [t1[0:1, :] + t2]
    for a in range(1, sub):
        pieces.append(jnp.where(row < k // (a + 1), t1[a:a + 1, :] + t2[0:sub, :], -jnp.inf))
    pieces.append(t1[sub:k, :] + t2[0:1, :])
    return jnp.concatenate(pieces, axis=0)


def _select_head(s1, s2):
    kk = PEER_TOPK
    t1, _ = _top_values(s1, kk)
    t2, rank2 = _top_values(s2, kk, with_rank=True)
    cand = _candidate_sums(t1, t2, kk)
    tau = _top_values(cand, kk)[0][kk - 1:kk, :]
    m1 = t1[0:1, :]
    m2 = t2[0:1, :]
    z = jnp.sum(jnp.where(cand >= tau, jnp.exp(cand - (m1 + m2)), 0.0), axis=0, keepdims=True)
    count = jnp.zeros(s1.shape, F32)
    for b in range(kk):
        count = count + jnp.where(s1 + t2[b:b + 1, :] >= tau, 1.0, 0.0)
    return (pltpu.bitcast(rank2.astype(BF16), jnp.uint32),
            pltpu.bitcast(jnp.exp(s2 - m2).astype(BF16), jnp.uint32),
            count, jnp.exp(s1 - m1) / z)


def _merge_select_kernel(x_ref, ya_ref, yb_ref, gt_ref, wa_ref, wb_ref, wo_ref, gain_ref,
                         wq_ref, sk_ref, h_ref, hnt_ref, rank_ref, e2_ref, n_ref, c_ref, sc_ref):
    d = x_ref.shape[1]

    @pl.when(pl.program_id(0) == 0)
    def _():
        sc_ref[...] = jnp.zeros_like(sc_ref)

    pa = jnp.dot(ya_ref[...], wa_ref[...], preferred_element_type=F32)
    pb = jnp.dot(yb_ref[...], wb_ref[...], preferred_element_type=F32)
    gates = gt_ref[...].astype(F32)
    merged = jax.nn.sigmoid(gates[:, :d]) * pa + jax.nn.sigmoid(gates[:, d:]) * pb
    h = x_ref[...] + jnp.dot(merged.astype(BF16), wo_ref[...], preferred_element_type=F32)
    h_ref[...] = h
    ms = jnp.mean(h * h, axis=-1, keepdims=True)
    hn_f32 = h * lax.rsqrt(ms + EPS) * gain_ref[...]
    hn = hn_f32.astype(BF16)
    hnt_ref[...] = pltpu.bitcast(hn_f32.T.astype(BF16), jnp.uint32)
    q = jnp.dot(hn, wq_ref[...], preferred_element_type=F32)
    hd = PEER_HALF_DIM
    for head in range(PEER_HEADS):
        rank_ref[head], e2_ref[head], n_ref[head], c_ref[head] = _select_head(
            sc_ref[2 * head], sc_ref[2 * head + 1])
        for hp in (2 * head, 2 * head + 1):
            sc_ref[hp] = lax.dot_general(sk_ref[hp], q[:, hp * hd:(hp + 1) * hd], NT_DIMS,
                                         preferred_element_type=F32,
                                         precision=lax.Precision.HIGHEST)


def _merge_select(x2, ya, yb, gates, wa, wb, wo, gain, wq, sk, tm):
    t, d = x2.shape
    nq = wq.shape[1]
    nhp, nk, hd = sk.shape
    nh = nhp // 2
    n_tiles = t // tm
    cur = lambda i: (jnp.minimum(i, n_tiles - 1), 0)
    prev = lambda i: (0, 0, jnp.maximum(i - 1, 0))
    tok = lambda w: pl.BlockSpec((tm, w), cur)
    return pl.pallas_call(
        _merge_select_kernel,
        grid=(n_tiles + 1,),
        in_specs=[tok(d), tok(ya.shape[1]), tok(yb.shape[1]), tok(2 * d),
                  _const_spec(wa.shape), _const_spec(wb.shape), _const_spec(wo.shape),
                  _const_spec((1, d)), _const_spec((d, nq)), _const_spec((nhp, nk, hd))],
        out_specs=[tok(d), pl.BlockSpec((d // 2, tm), lambda i: (0, jnp.minimum(i, n_tiles - 1))),
                   pl.BlockSpec((nh, nk // 2, tm), prev), pl.BlockSpec((nh, nk // 2, tm), prev),
                   pl.BlockSpec((nh, nk, tm), prev), pl.BlockSpec((nh, nk, tm), prev)],
        out_shape=[jax.ShapeDtypeStruct((t, d), F32),
                   jax.ShapeDtypeStruct((d // 2, t), jnp.uint32),
                   jax.ShapeDtypeStruct((nh, nk // 2, t), jnp.uint32),
                   jax.ShapeDtypeStruct((nh, nk // 2, t), jnp.uint32),
                   jax.ShapeDtypeStruct((nh, nk, t), F32),
                   jax.ShapeDtypeStruct((nh, nk, t), F32)],
        scratch_shapes=[pltpu.VMEM((nhp, nk, tm), F32)],
        compiler_params=_cparams(("arbitrary",)),
        name="merge_select",
    )(x2, ya, yb, gates, wa, wb, wo, gain, wq, sk)


def _gated_activation(rank_ref, e2_ref, n_ref, c_ref, act_ref, w_ref, row0, n_rows):
    nk = PEER_N_KEYS
    half = nk // 2
    pack = 16
    tm = act_ref.shape[1]
    for tc in range(tm // LANES):
        lanes = slice(tc * LANES, (tc + 1) * LANES)
        for hf in range(2):
            words = slice(hf * half // 2, (hf + 1) * half // 2)
            gates = [None] * n_rows
            for h in range(PEER_HEADS):
                rk = pltpu.bitcast(rank_ref[h, words, lanes], BF16)
                ev = pltpu.bitcast(e2_ref[h, words, lanes], BF16)
                for ii in range(n_rows):
                    i1 = row0 + ii
                    cnt = jnp.broadcast_to(n_ref[h, i1:i1 + 1, lanes], (pack, LANES)).astype(BF16)
                    cf = jnp.broadcast_to(c_ref[h, i1:i1 + 1, lanes], (pack, LANES)).astype(BF16)
                    cnt = jnp.concatenate([cnt] * (half // pack), axis=0)
                    cf = jnp.concatenate([cf] * (half // pack), axis=0)
                    term = jnp.where(rk < cnt, ev * cf, jnp.zeros_like(ev))
                    gates[ii] = term if gates[ii] is None else gates[ii] + term
            for ii in range(n_rows):
                rows = slice(ii * nk + hf * half, ii * nk + (hf + 1) * half)
                a = act_ref[rows, lanes]
                gelu = 0.5 * a * (1.0 + lax.erf(a * (2.0 ** -0.5)))
                w_ref[rows, lanes] = gates[ii] * gelu.astype(BF16)


def _peer_dense_kernel(hnt_ref, h_ref, dn_ref, upt_ref, rank_ref, e2_ref, n_ref, c_ref,
                       o_ref, act_ref, w_ref, acc_ref):
    e = pl.program_id(1)
    n_split, ebs, _ = act_ref.shape
    rows = ebs // PEER_N_KEYS

    @pl.when(e == 0)
    def _():
        acc_ref[...] = jnp.zeros_like(acc_ref)

    hnt = pltpu.bitcast(hnt_ref[...], BF16)
    for s in range(n_split):
        dn = pltpu.bitcast(dn_ref[s * ebs // 2:(s + 1) * ebs // 2, :], BF16)
        act_ref[s] = jnp.dot(dn, hnt, preferred_element_type=F32)
    for s in range(n_split):
        _gated_activation(rank_ref, e2_ref, n_ref, c_ref, act_ref.at[s], w_ref.at[s],
                          s * rows, rows)
    w = w_ref[...].reshape(n_split * ebs, w_ref.shape[2])
    acc_ref[...] += jnp.dot(pltpu.bitcast(upt_ref[...], BF16), w,
                            preferred_element_type=F32)

    @pl.when(e == pl.num_programs(1) - 1)
    def _():
        o_ref[...] = h_ref[...] + acc_ref[...].T


def _peer_dense(hnt, h, dn, upt, rank2, e2, count, coef, tm, eb, n_split):
    t, d = h.shape
    ne = upt.shape[1]
    nk = PEER_N_KEYS
    nh = PEER_HEADS
    rows = eb // nk
    ebs = eb // n_split
    return pl.pallas_call(
        _peer_dense_kernel,
        grid=(t // tm, ne // eb),
        in_specs=[
            pl.BlockSpec((d // 2, tm), lambda i, e: (0, i)),
            pl.BlockSpec((tm, d), lambda i, e: (i, 0)),
            pl.BlockSpec((eb // 2, d), lambda i, e: (e, 0)),
            pl.BlockSpec((d // 2, eb), lambda i, e: (0, e)),
            pl.BlockSpec((nh, nk // 2, tm), lambda i, e: (0, 0, i)),
            pl.BlockSpec((nh, nk // 2, tm), lambda i, e: (0, 0, i)),
            pl.BlockSpec((nh, rows, tm), lambda i, e: (0, e, i)),
            pl.BlockSpec((nh, rows, tm), lambda i, e: (0, e, i)),
        ],
        out_specs=pl.BlockSpec((tm, d), lambda i, e: (i, 0)),
        out_shape=jax.ShapeDtypeStruct((t, d), F32),
        scratch_shapes=[pltpu.VMEM((n_split, ebs, tm), F32),
                        pltpu.VMEM((n_split, ebs, tm), BF16),
                        pltpu.VMEM((d, tm), F32)],
        compiler_params=_cparams(("parallel", "arbitrary")),
        name="peer_dense",
    )(hnt, h, dn, upt, rank2, e2, count, coef)


def _duplicate_heads(w, n_heads):
    d = w.shape[0]
    w3 = w.reshape(d, n_heads, -1)
    return jnp.concatenate([w3, w3], axis=2).reshape(d, -1)


def _layer(h2, batch, seq, p):
    t, d = h2.shape
    hk, hv = GLA_HEADS * GLA_DK, GLA_HEADS * GLA_DV
    nq, nkv = SWA_Q_HEADS * SWA_HEAD_DIM, SWA_KV_HEADS * SWA_HEAD_DIM
    rank = GLA_GATE_RANK
    tm = min(512, t)
    w_in = p["w_in"]
    bounds = np.cumsum([0, 2 * hk, hv, hv, rank, nq, nkv, nkv, 2 * d])
    assert bounds[-1] == w_in.shape[1]
    w_qk, w_v, w_r, w_ga, w_sq, w_sk, w_sv, w_gt = (
        w_in[:, a:b] for a, b in zip(bounds[:-1], bounds[1:]))
    w_cat = jnp.concatenate(
        [w_qk, w_v, w_r, w_sq, _duplicate_heads(w_sk, SWA_KV_HEADS),
         _duplicate_heads(w_sv, SWA_KV_HEADS), w_gt], axis=1).astype(BF16)
    widths = (2 * hk, hv, hv, nq, 2 * nkv, 2 * nkv, 2 * d)
    w_ga = jnp.pad(w_ga, ((0, 0), (0, LANES - rank))).astype(BF16)
    w_up = jnp.pad(p["w_gate_up"], ((0, LANES - rank), (0, 0)))
    qk, gv, gr, sq, skd, svd, gates, g = _in_proj(
        h2, p["norm_mix_gain"].reshape(1, d), w_cat, w_ga, w_up,
        p["b_gate_up"].reshape(1, hk), widths, tm)

    y_gla = _gla(qk, gv, gr, g, p["gla_out_gain"].reshape(1, GLA_DV), batch, seq, min(512, seq))

    one_hot = (np.arange(REL_BUCKETS)[:, None] == _t5_bucket_table().reshape(1, -1)).astype(np.float32)
    bias = jnp.dot(p["rel_bias"].astype(F32).T, one_hot, precision=lax.Precision.HIGHEST)
    bias = bias.reshape(SWA_Q_HEADS, SWA_WINDOW, 2 * SWA_WINDOW)
    bias = jnp.where(_window_mask()[None], bias, -jnp.inf)
    bias = bias.reshape(SWA_Q_HEADS // 2, 2, SWA_WINDOW, 2 * SWA_WINDOW)
    bias = jnp.transpose(bias, (0, 2, 1, 3)).reshape(SWA_Q_HEADS // 2, SWA_WINDOW, 4 * SWA_WINDOW)
    q_scale = jnp.tile(p["swa_q_gain"], SWA_Q_HEADS).reshape(1, nq) * (SWA_HEAD_DIM ** -0.5)
    k_scale = jnp.tile(p["swa_k_gain"], 2 * SWA_KV_HEADS).reshape(1, 2 * nkv)
    y_swa = _swa(sq, skd, svd, bias, q_scale, k_scale,
                 p["swa_sinks"].reshape(1, SWA_Q_HEADS), batch, seq)

    sk = p["peer_sub_keys"].reshape(2 * PEER_HEADS, PEER_N_KEYS, PEER_HALF_DIM)
    h_mid, hnt, rank2, e2, count, coef = _merge_select(
        h2, y_gla, y_swa, gates, p["w_branch_gla"].astype(BF16),
        p["w_branch_swa"].astype(BF16), p["w_out"].astype(BF16),
        p["norm_ffn_gain"].reshape(1, d), p["peer_w_query"].astype(BF16), sk, min(256, t))
    dn = _pack_rows(p["peer_expert_down"], 1024)
    upt = _pack_rows(p["peer_expert_up"].T, 64)
    return _peer_dense(hnt, h_mid, dn, upt, rank2, e2, count, coef, tm, 2048, 2)


def kernel(x, norm_mix_gain, w_in, w_gate_up, b_gate_up, gla_out_gain, swa_q_gain,
           swa_k_gain, swa_sinks, rel_bias, w_branch_gla, w_branch_swa, w_out,
           norm_ffn_gain, peer_w_query, peer_sub_keys, peer_expert_down, peer_expert_up):
    batch, seq, d = x.shape
    h2 = x.reshape(batch * seq, d)
    for layer in range(w_in.shape[0]):
        p = dict(norm_mix_gain=norm_mix_gain[layer], w_in=w_in[layer],
                 w_gate_up=w_gate_up[layer], b_gate_up=b_gate_up[layer],
                 gla_out_gain=gla_out_gain[layer], swa_q_gain=swa_q_gain[layer],
                 swa_k_gain=swa_k_gain[layer], swa_sinks=swa_sinks[layer],
                 rel_bias=rel_bias, w_branch_gla=w_branch_gla[layer],
                 w_branch_swa=w_branch_swa[layer], w_out=w_out[layer],
                 norm_ffn_gain=norm_ffn_gain[layer], peer_w_query=peer_w_query[layer],
                 peer_sub_keys=peer_sub_keys[layer], peer_expert_down=peer_expert_down[layer],
                 peer_expert_up=peer_expert_up[layer])
        h2 = _layer(h2, batch, seq, p)
    return h2.reshape(batch, seq, d)
```

```python
import functools
import math

import numpy as np
import jax
import jax.numpy as jnp
from jax import lax
from jax.experimental import pallas as pl
from jax.experimental.pallas import tpu as pltpu

F32 = jnp.float32
BF16 = jnp.bfloat16

EPS = 1e-6

GLA_HEADS = 4
GLA_DK = 128
GLA_DV = 256
GLA_GATE_RANK = 16
GLA_GATE_NORMALIZER = 16.0
GLA_CHUNK = 64
GLA_SUB = 16

SWA_Q_HEADS = 16
SWA_KV_HEADS = 4
SWA_HEAD_DIM = 64
SWA_WINDOW = 128
REL_BUCKETS = 32
REL_MAX_DISTANCE = 128

PEER_HEADS = 8
PEER_N_KEYS = 128
PEER_HALF_DIM = 128
PEER_TOPK = 16

LANES = 128
VMEM_LIMIT = 56 * 1024 * 1024

NT_DIMS = (((1,), (1,)), ((), ()))
TN_DIMS = (((0,), (0,)), ((), ()))


def _cparams(semantics):
    return pltpu.CompilerParams(dimension_semantics=semantics,
                                vmem_limit_bytes=VMEM_LIMIT)


def _const_spec(shape):
    nd = len(shape)
    return pl.BlockSpec(shape, lambda *_: (0,) * nd)


def _split_bf16(x):
    hi = x.astype(BF16)
    lo = (x - hi.astype(F32)).astype(BF16)
    return hi, lo


def _pack_rows_kernel(x_ref, o_ref):
    o_ref[...] = pltpu.bitcast(x_ref[...].astype(BF16), jnp.uint32)


def _pack_rows(x, tr):
    r, c = x.shape
    return pl.pallas_call(
        _pack_rows_kernel,
        grid=(r // tr,),
        in_specs=[pl.BlockSpec((tr, c), lambda i: (i, 0))],
        out_specs=pl.BlockSpec((tr // 2, c), lambda i: (i, 0)),
        out_shape=jax.ShapeDtypeStruct((r // 2, c), jnp.uint32),
        compiler_params=_cparams(("parallel",)),
        name="pack_rows",
    )(x)


def _in_proj_kernel(x_ref, g_ref, w_ref, wga_ref, wup_ref, bup_ref, *out_refs):
    x = x_ref[...]
    ms = jnp.mean(x * x, axis=-1, keepdims=True)
    xn = (x * lax.rsqrt(ms + EPS) * g_ref[...]).astype(BF16)
    off = 0
    for o_ref in out_refs[:-1]:
        width = o_ref.shape[1]
        o_ref[...] = jnp.dot(xn, w_ref[:, off:off + width],
                             preferred_element_type=F32).astype(o_ref.dtype)
        off += width
    a_low = jnp.dot(xn, wga_ref[...], preferred_element_type=F32)
    pre = jnp.dot(a_low, wup_ref[...], preferred_element_type=F32,
                  precision=lax.Precision.HIGHEST) + bup_ref[...]
    out_refs[-1][...] = jax.nn.log_sigmoid(pre) * (1.0 / GLA_GATE_NORMALIZER)


def _in_proj(x2, gain, w_cat, w_ga, w_up, b_up, widths, tm):
    t, d = x2.shape
    nw = w_cat.shape[1]
    ng = w_up.shape[1]
    out_shape = [jax.ShapeDtypeStruct((t, w), BF16) for w in widths]
    out_shape.append(jax.ShapeDtypeStruct((t, ng), F32))
    out_specs = [pl.BlockSpec((tm, w), lambda i: (i, 0)) for w in widths + (ng,)]
    return pl.pallas_call(
        _in_proj_kernel,
        grid=(t // tm,),
        in_specs=[pl.BlockSpec((tm, d), lambda i: (i, 0)),
                  _const_spec((1, d)),
                  _const_spec((d, nw)),
                  _const_spec(w_ga.shape),
                  _const_spec(w_up.shape),
                  _const_spec((1, ng))],
        out_specs=out_specs,
        out_shape=out_shape,
        compiler_params=_cparams(("parallel",)),
        name="in_proj",
    )(x2, gain, w_cat, w_ga, w_up, b_up)


def _gla_chunk_head(q, k, v, g, st, tril):
    c_rows, sub = GLA_CHUNK, GLA_SUB
    row_s = lax.broadcasted_iota(jnp.int32, (c_rows, sub), 0)
    sub_j = lax.broadcasted_iota(jnp.int32, (sub, sub), 1)
    sub_r = lax.broadcasted_iota(jnp.int32, (sub, 1), 0)
    g_hi, g_lo = _split_bf16(g)
    b = (jnp.dot(tril, g_hi, preferred_element_type=F32)
         + jnp.dot(tril, g_lo, preferred_element_type=F32))
    o = lax.dot_general((q * jnp.exp(b)).astype(BF16), st.astype(BF16), NT_DIMS,
                        preferred_element_type=F32)
    for j in range(c_rows // sub):
        lo, hi = j * sub, (j + 1) * sub
        b_j, k_j, q_j = b[lo:hi, :], k[lo:hi, :], q[lo:hi, :]
        b_end = b[hi - 1:hi, :]
        diag = jnp.zeros((sub, sub), F32)
        for jj in range(sub):
            dec = jnp.exp(jnp.minimum(b_j - b_j[jj:jj + 1, :], 0.0))
            colv = jnp.sum(q_j * k_j[jj:jj + 1, :] * dec, axis=-1, keepdims=True)
            colv = jnp.where(sub_r >= jj, colv, 0.0)
            diag = jnp.where(sub_j == jj, colv, diag)
        pieces = []
        if lo > 0:
            pieces.append(jnp.zeros((lo, sub), F32))
        pieces.append(diag)
        if hi < c_rows:
            pieces.append(jnp.zeros((c_rows - hi, sub), F32))
        s_col = jnp.concatenate(pieces, axis=0)
        if hi < c_rows:
            k_t = (k_j * jnp.exp(b_end - b_j)).astype(BF16)
            q_t = (q * jnp.exp(jnp.minimum(b - b_end, 0.0))).astype(BF16)
            s_off = lax.dot_general(q_t, k_t, NT_DIMS, preferred_element_type=F32)
            s_col = s_col + jnp.where(row_s >= hi, s_off, 0.0)
        o = o + jnp.dot(s_col.astype(BF16), v[lo:hi, :], preferred_element_type=F32)
    b_last = b[c_rows - 1:c_rows, :]
    k_s = (k * jnp.exp(b_last - b)).astype(BF16)
    upd = lax.dot_general(v, k_s, TN_DIMS, preferred_element_type=F32)
    return o, st * jnp.exp(b_last) + upd


def _gla_kernel(qk_ref, v_ref, r_ref, g_ref, gain_ref, o_ref, state_ref, *, n_chunks):
    c_rows = GLA_CHUNK
    nh, dk, dv = GLA_HEADS, GLA_DK, GLA_DV

    @pl.when(pl.program_id(1) == 0)
    def _():
        state_ref[...] = jnp.zeros_like(state_ref)

    row = lax.broadcasted_iota(jnp.int32, (c_rows, c_rows), 0)
    col = lax.broadcasted_iota(jnp.int32, (c_rows, c_rows), 1)
    tril = (row >= col).astype(BF16)

    def chunk(c, carry):
        rows = pl.ds(pl.multiple_of(c * c_rows, c_rows), c_rows)
        for h in range(nh):
            q = qk_ref[rows, h * dk:(h + 1) * dk].astype(F32) * (dk ** -0.5)
            k = qk_ref[rows, (nh + h) * dk:(nh + h + 1) * dk].astype(F32)
            v = v_ref[rows, h * dv:(h + 1) * dv]
            g = g_ref[rows, h * dk:(h + 1) * dk]
            o, st = _gla_chunk_head(q, k, v, g, state_ref[h], tril)
            state_ref[h] = st
            ms = jnp.mean(o * o, axis=-1, keepdims=True)
            on = o * lax.rsqrt(ms + EPS) * gain_ref[...]
            r = r_ref[rows, h * dv:(h + 1) * dv].astype(F32)
            o_ref[rows, h * dv:(h + 1) * dv] = (on * (r * jax.nn.sigmoid(r))).astype(o_ref.dtype)
        return carry

    lax.fori_loop(0, n_chunks, chunk, 0)


def _gla(qk, gv, gr, g, out_gain, batch, seq, ts):
    nh, dk, dv = GLA_HEADS, GLA_DK, GLA_DV
    tok = lambda w: pl.BlockSpec((None, ts, w), lambda b, s: (b, s, 0))
    kern = functools.partial(_gla_kernel, n_chunks=ts // GLA_CHUNK)
    out = pl.pallas_call(
        kern,
        grid=(batch, seq // ts),
        in_specs=[tok(2 * nh * dk), tok(nh * dv), tok(nh * dv), tok(nh * dk),
                  _const_spec((1, dv))],
        out_specs=tok(nh * dv),
        out_shape=jax.ShapeDtypeStruct((batch, seq, nh * dv), BF16),
        scratch_shapes=[pltpu.VMEM((nh, dv, dk), F32)],
        compiler_params=_cparams(("parallel", "arbitrary")),
        name="gla",
    )(qk.reshape(batch, seq, -1), gv.reshape(batch, seq, -1), gr.reshape(batch, seq, -1),
      g.reshape(batch, seq, -1), out_gain)
    return out.reshape(batch * seq, nh * dv)


def _t5_bucket_table():
    w = SWA_WINDOW
    qi = np.arange(w)[:, None]
    kj = np.arange(2 * w)[None, :]
    rel = np.clip(qi + w - kj, 0, None)
    max_exact = REL_BUCKETS // 2
    rel_f = np.maximum(rel, 1).astype(np.float32)
    large = max_exact + (np.log(rel_f / np.float32(max_exact))
                         / np.float32(math.log(REL_MAX_DISTANCE / max_exact))
                         * np.float32(REL_BUCKETS - max_exact)).astype(np.int32)
    large = np.minimum(large, REL_BUCKETS - 1)
    return np.where(rel < max_exact, rel, large).astype(np.int32)


def _window_mask():
    w = SWA_WINDOW
    rel = np.arange(w)[:, None] + w - np.arange(2 * w)[None, :]
    return (rel >= 0) & (rel < w)


def _segment_matrix(width):
    seg = np.arange(width)[:, None] // SWA_HEAD_DIM
    return (seg == np.arange(LANES)[None, :]).astype(np.float32)


def _segment_rsqrt(x, e_ref, et_ref):
    hi, lo = _split_bf16(x * x)
    ss = (jnp.dot(hi, e_ref[...], preferred_element_type=F32)
          + jnp.dot(lo, e_ref[...], preferred_element_type=F32))
    hi, lo = _split_bf16(lax.rsqrt(ss * (1.0 / SWA_HEAD_DIM) + EPS))
    return (jnp.dot(hi, et_ref[...], preferred_element_type=F32)
            + jnp.dot(lo, et_ref[...], preferred_element_type=F32))


def _swa_kernel(q_ref, kp_ref, kc_ref, vp_ref, vc_ref, bias_ref, qs_ref, ks_ref, sink_ref,
                eq_ref, eqt_ref, ek_ref, ekt_ref, o_ref):
    w, hd = SWA_WINDOW, SWA_HEAD_DIM
    group = SWA_Q_HEADS // SWA_KV_HEADS
    n = pl.program_id(1)
    band = 2 * w
    n_pairs = SWA_Q_HEADS // 2
    first = (n == 0).astype(jnp.int32)
    low_k = lax.broadcasted_iota(jnp.int32, (band, LANES), 1) < hd
    low_q = lax.broadcasted_iota(jnp.int32, (w, LANES), 1) < hd

    q = q_ref[...].astype(F32)
    qn = (q * _segment_rsqrt(q, eq_ref, eqt_ref) * qs_ref[...]).astype(BF16)
    kd = jnp.concatenate([kp_ref[...], kc_ref[...]], axis=0).astype(F32)
    kn = (kd * _segment_rsqrt(kd, ek_ref, ekt_ref) * ks_ref[...]).astype(BF16)
    vd = jnp.concatenate([vp_ref[...], vc_ref[...]], axis=0)
    zero = jnp.zeros((band, LANES), BF16)
    ones_low = jnp.where(low_k, 1.0, 0.0).astype(BF16)
    ones_high = jnp.where(low_k, 0.0, 1.0).astype(BF16)
    k_both, v_both = [], []
    for g in range(SWA_KV_HEADS):
        kg = kn[:, g * LANES:(g + 1) * LANES]
        vg = vd[:, g * LANES:(g + 1) * LANES]
        k_both.append(jnp.concatenate([jnp.where(low_k, kg, zero), jnp.where(low_k, zero, kg)], axis=0))
        v_both.append(jnp.concatenate(
            [jnp.concatenate([jnp.where(low_k, vg, zero), ones_low], axis=1),
             jnp.concatenate([jnp.where(low_k, zero, vg), ones_high], axis=1)],
            axis=0))

    pairs_per_group = group // 2

    def logits_of(g):
        qs = jnp.concatenate([qn[:, p * LANES:(p + 1) * LANES]
                              for p in range(g * pairs_per_group, (g + 1) * pairs_per_group)], axis=0)
        lg = lax.dot_general(qs, k_both[g], NT_DIMS, preferred_element_type=F32)
        return lg + bias_ref[first, g]

    def probs_of(g, lg):
        es, sink_es = [], []
        for p in range(pairs_per_group):
            rows = slice(p * w, (p + 1) * w)
            e_p, s_p = [], []
            for half in range(2):
                hq = 2 * (g * pairs_per_group + p) + half
                sink = sink_ref[hq]
                lh = lg[rows, half * band:(half + 1) * band]
                m = jnp.maximum(jnp.broadcast_to(jnp.max(lh, axis=-1, keepdims=True), (w, LANES)),
                                sink)
                e_p.append(jnp.exp(lh - jnp.concatenate([m] * (band // LANES), axis=1)).astype(BF16))
                s_p.append(jnp.exp(sink - m))
            es.append(jnp.concatenate(e_p, axis=1))
            sink_es.append(jnp.where(low_q, s_p[0], s_p[1]))
        return jnp.concatenate(es, axis=0), jnp.concatenate(sink_es, axis=0)

    def out_of(g, e, sink_e):
        pv = jnp.dot(e, v_both[g], preferred_element_type=F32)
        return pv[:, :LANES] / (pv[:, LANES:] + sink_e)

    lead = 2
    n_groups = SWA_KV_HEADS
    logits, probs = {}, {}
    for step in range(n_groups + lead):
        if step < n_groups:
            logits[step] = logits_of(step)
        if 1 <= step <= n_groups:
            probs[step - 1] = probs_of(step - 1, logits.pop(step - 1))
        if step >= lead:
            g = step - lead
            res = out_of(g, *probs.pop(g)).astype(o_ref.dtype)
            for p in range(pairs_per_group):
                grp = g * pairs_per_group + p
                o_ref[:, grp * LANES:(grp + 1) * LANES] = res[p * w:(p + 1) * w, :]


def _swa(sq, skd, svd, bias, q_scale, k_scale, sinks, batch, seq):
    w, hd = SWA_WINDOW, SWA_HEAD_DIM
    hq, hkv = SWA_Q_HEADS, SWA_KV_HEADS
    nq, nkd = hq * hd, hkv * LANES
    eq = _segment_matrix(nq)
    ek = _segment_matrix(nkd)
    prev = lambda b, n: (b, jnp.maximum(n - 1, 0), 0)
    cur = lambda b, n: (b, n, 0)
    out = pl.pallas_call(
        _swa_kernel,
        grid=(batch, seq // w),
        in_specs=[
            pl.BlockSpec((None, w, nq), cur),
            pl.BlockSpec((None, w, nkd), prev),
            pl.BlockSpec((None, w, nkd), cur),
            pl.BlockSpec((None, w, nkd), prev),
            pl.BlockSpec((None, w, nkd), cur),
            _const_spec((2, hkv, (hq // hkv // 2) * w, 4 * w)),
            _const_spec((1, nq)),
            _const_spec((1, nkd)),
            _const_spec((hq, 1, LANES)),
            _const_spec(eq.shape), _const_spec(eq.T.shape),
            _const_spec(ek.shape), _const_spec(ek.T.shape),
        ],
        out_specs=pl.BlockSpec((None, w, nq), cur),
        out_shape=jax.ShapeDtypeStruct((batch, seq, nq), BF16),
        compiler_params=_cparams(("parallel", "parallel")),
        name="swa",
    )(sq.reshape(batch, seq, nq), *([skd.reshape(batch, seq, nkd)] * 2),
      *([svd.reshape(batch, seq, nkd)] * 2), bias, q_scale, k_scale, sinks,
      jnp.asarray(eq, BF16), jnp.asarray(eq.T, BF16), jnp.asarray(ek, BF16), jnp.asarray(ek.T, BF16))
    return out.reshape(batch * seq, nq)


def _top_values(s, k, with_rank=False):
    t = s.shape[1]
    row = lax.broadcasted_iota(jnp.int32, (k, t), 0)
    vals = jnp.zeros((k, t), F32)
    rank = jnp.full(s.shape, float(k), F32) if with_rank else None
    cur = s
    for r in range(k):
        m = jnp.max(cur, axis=0, keepdims=True)
        vals = jnp.where(row == r, m, vals)
        hit = cur == m
        if with_rank:
            rank = jnp.where(hit, float(r), rank)
        if r + 1 < k:
            cur = jnp.where(hit, -jnp.inf, cur)
    return vals, rank


def _candidate_sums(t1, t2, k):
    assert k == 16
    sub = 8
    row = lax.broadcasted_iota(jnp.int32, (sub, t1.shape[1]), 0)
    pieces = [t1[0:1, :] + t2]
    for a in range(1, sub):
        pieces.append(jnp.where(row < k // (a + 1), t1[a:a + 1, :] + t2[0:sub, :], -jnp.inf))
    pieces.append(t1[sub:k, :] + t2[0:1, :])
    return jnp.concatenate(pieces, axis=0)


def _select_head(s1, s2):
    kk = PEER_TOPK
    t1, _ = _top_values(s1, kk)
    t2, rank2 = _top_values(s2, kk, with_rank=True)
    cand = _candidate_sums(t1, t2, kk)
    tau = _top_values(cand, kk)[0][kk - 1:kk, :]
    m1 = t1[0:1, :]
    m2 = t2[0:1, :]
    z = jnp.sum(jnp.where(cand >= tau, jnp.exp(cand - (m1 + m2)), 0.0), axis=0, keepdims=True)
    count = jnp.zeros(s1.shape, F32)
    for b in range(kk):
        count = jnp.where(s1 + t2[b:b + 1, :] >= tau, float(b + 1), count)
    return (pltpu.bitcast(rank2.astype(BF16), jnp.uint32),
            pltpu.bitcast(jnp.exp(s2 - m2).astype(BF16), jnp.uint32),
            count, jnp.exp(s1 - m1) / z)


def _merge_select_kernel(x_ref, ya_ref, yb_ref, gt_ref, wa_ref, wb_ref, wo_ref, gain_ref,
                         wq_ref, sk_ref, h_ref, hnt_ref, rank_ref, e2_ref, n_ref, c_ref, sc_ref):
    d = x_ref.shape[1]

    @pl.when(pl.program_id(0) == 0)
    def _():
        sc_ref[...] = jnp.zeros_like(sc_ref)

    pa = jnp.dot(ya_ref[...], wa_ref[...], preferred_element_type=F32)
    pb = jnp.dot(yb_ref[...], wb_ref[...], preferred_element_type=F32)
    gates = gt_ref[...].astype(F32)
    merged = jax.nn.sigmoid(gates[:, :d]) * pa + jax.nn.sigmoid(gates[:, d:]) * pb
    h = x_ref[...] + jnp.dot(merged.astype(BF16), wo_ref[...], preferred_element_type=F32)
    h_ref[...] = h
    ms = jnp.mean(h * h, axis=-1, keepdims=True)
    hn_f32 = h * lax.rsqrt(ms + EPS) * gain_ref[...]
    hn = hn_f32.astype(BF16)
    hnt_ref[...] = pltpu.bitcast(hn_f32.T.astype(BF16), jnp.uint32)
    q = jnp.dot(hn, wq_ref[...], preferred_element_type=F32)
    hd = PEER_HALF_DIM
    for head in range(PEER_HEADS):
        rank_ref[head], e2_ref[head], n_ref[head], c_ref[head] = _select_head(
            sc_ref[2 * head], sc_ref[2 * head + 1])
        for hp in (2 * head, 2 * head + 1):
            sc_ref[hp] = lax.dot_general(sk_ref[hp], q[:, hp * hd:(hp + 1) * hd], NT_DIMS,
                                         preferred_element_type=F32,
                                         precision=lax.Precision.HIGHEST)


def _merge_select(x2, ya, yb, gates, wa, wb, wo, gain, wq, sk, tm):
    t, d = x2.shape
    nq = wq.shape[1]
    nhp, nk, hd = sk.shape
    nh = nhp // 2
    n_tiles = t // tm
    cur = lambda i: (jnp.minimum(i, n_tiles - 1), 0)
    prev = lambda i: (0, 0, jnp.maximum(i - 1, 0))
    tok = lambda w: pl.BlockSpec((tm, w), cur)
    return pl.pallas_call(
        _merge_select_kernel,
        grid=(n_tiles + 1,),
        in_specs=[tok(d), tok(ya.shape[1]), tok(yb.shape[1]), tok(2 * d),
                  _const_spec(wa.shape), _const_spec(wb.shape), _const_spec(wo.shape),
                  _const_spec((1, d)), _const_spec((d, nq)), _const_spec((nhp, nk, hd))],
        out_specs=[tok(d), pl.BlockSpec((d // 2, tm), lambda i: (0, jnp.minimum(i, n_tiles - 1))),
                   pl.BlockSpec((nh, nk // 2, tm), prev), pl.BlockSpec((nh, nk // 2, tm), prev),
                   pl.BlockSpec((nh, nk, tm), prev), pl.BlockSpec((nh, nk, tm), prev)],
        out_shape=[jax.ShapeDtypeStruct((t, d), F32),
                   jax.ShapeDtypeStruct((d // 2, t), jnp.uint32),
                   jax.ShapeDtypeStruct((nh, nk // 2, t), jnp.uint32),
                   jax.ShapeDtypeStruct((nh, nk // 2, t), jnp.uint32),
                   jax.ShapeDtypeStruct((nh, nk, t), F32),
                   jax.ShapeDtypeStruct((nh, nk, t), F32)],
        scratch_shapes=[pltpu.VMEM((nhp, nk, tm), F32)],
        compiler_params=_cparams(("arbitrary",)),
        name="merge_select",
    )(x2, ya, yb, gates, wa, wb, wo, gain, wq, sk)


def _gated_activation(rank_ref, e2_ref, n_ref, c_ref, act_ref, w_ref, row0, n_rows):
    nk = PEER_N_KEYS
    half = nk // 2
    pack = 16
    tm = act_ref.shape[1]
    for tc in range(tm // LANES):
        lanes = slice(tc * LANES, (tc + 1) * LANES)
        for hf in range(2):
            words = slice(hf * half // 2, (hf + 1) * half // 2)
            gates = [None] * n_rows
            for h in range(PEER_HEADS):
                rk = pltpu.bitcast(rank_ref[h, words, lanes], BF16)
                ev = pltpu.bitcast(e2_ref[h, words, lanes], BF16)
                for ii in range(n_rows):
                    i1 = row0 + ii
                    cnt = jnp.broadcast_to(n_ref[h, i1:i1 + 1, lanes], (pack, LANES)).astype(BF16)
                    cf = jnp.broadcast_to(c_ref[h, i1:i1 + 1, lanes], (pack, LANES)).astype(BF16)
                    cnt = jnp.concatenate([cnt] * (half // pack), axis=0)
                    cf = jnp.concatenate([cf] * (half // pack), axis=0)
                    term = jnp.where(rk < cnt, ev * cf, jnp.zeros_like(ev))
                    gates[ii] = term if gates[ii] is None else gates[ii] + term
            for ii in range(n_rows):
                rows = slice(ii * nk + hf * half, ii * nk + (hf + 1) * half)
                a = act_ref[rows, lanes]
                gelu = 0.5 * a * (1.0 + lax.erf(a * (2.0 ** -0.5)))
                w_ref[rows, lanes] = gates[ii] * gelu.astype(BF16)


def _peer_dense_kernel(hnt_ref, h_ref, dn_ref, upt_ref, rank_ref, e2_ref, n_ref, c_ref,
                       o_ref, act_ref, w_ref, acc_ref):
    e = pl.program_id(1)
    n_split, ebs, _ = act_ref.shape
    rows = ebs // PEER_N_KEYS

    @pl.when(e == 0)
    def _():
        acc_ref[...] = jnp.zeros_like(acc_ref)

    hnt = pltpu.bitcast(hnt_ref[...], BF16)
    for s in range(n_split):
        dn = pltpu.bitcast(dn_ref[s * ebs // 2:(s + 1) * ebs // 2, :], BF16)
        act_ref[s] = jnp.dot(dn, hnt, preferred_element_type=F32)
    for s in range(n_split):
        _gated_activation(rank_ref, e2_ref, n_ref, c_ref, act_ref.at[s], w_ref.at[s],
                          s * rows, rows)
    w = w_ref[...].reshape(n_split * ebs, w_ref.shape[2])
    acc_ref[...] += jnp.dot(pltpu.bitcast(upt_ref[...], BF16), w,
                            preferred_element_type=F32)

    @pl.when(e == pl.num_programs(1) - 1)
    def _():
        o_ref[...] = h_ref[...] + acc_ref[...].T


def _peer_dense(hnt, h, dn, upt, rank2, e2, count, coef, tm, eb, n_split):
    t, d = h.shape
    ne = upt.shape[1]
    nk = PEER_N_KEYS
    nh = PEER_HEADS
    rows = eb // nk
    ebs = eb // n_split
    return pl.pallas_call(
        _peer_dense_kernel,
        grid=(t // tm, ne // eb),
        in_specs=[
            pl.BlockSpec((d // 2, tm), lambda i, e: (0, i)),
            pl.BlockSpec((tm, d), lambda i, e: (i, 0)),
            pl.BlockSpec((eb // 2, d), lambda i, e: (e, 0)),
            pl.BlockSpec((d // 2, eb), lambda i, e: (0, e)),
            pl.BlockSpec((nh, nk // 2, tm), lambda i, e: (0, 0, i)),
            pl.BlockSpec((nh, nk // 2, tm), lambda i, e: (0, 0, i)),
            pl.BlockSpec((nh, rows, tm), lambda i, e: (0, e, i)),
            pl.BlockSpec((nh, rows, tm), lambda i, e: (0, e, i)),
        ],
        out_specs=pl.BlockSpec((tm, d), lambda i, e: (i, 0)),
        out_shape=jax.ShapeDtypeStruct((t, d), F32),
        scratch_shapes=[pltpu.VMEM((n_split, ebs, tm), F32),
                        pltpu.VMEM((n_split, ebs, tm), BF16),
                        pltpu.VMEM((d, tm), F32)],
        compiler_params=_cparams(("parallel", "arbitrary")),
        name="peer_dense",
    )(hnt, h, dn, upt, rank2, e2, count, coef)


def _duplicate_heads(w, n_heads):
    d = w.shape[0]
    w3 = w.reshape(d, n_heads, -1)
    return jnp.concatenate([w3, w3], axis=2).reshape(d, -1)


def _layer(h2, batch, seq, p):
    t, d = h2.shape
    hk, hv = GLA_HEADS * GLA_DK, GLA_HEADS * GLA_DV
    nq, nkv = SWA_Q_HEADS * SWA_HEAD_DIM, SWA_KV_HEADS * SWA_HEAD_DIM
    rank = GLA_GATE_RANK
    tm = min(512, t)
    w_in = p["w_in"]
    bounds = np.cumsum([0, 2 * hk, hv, hv, rank, nq, nkv, nkv, 2 * d])
    assert bounds[-1] == w_in.shape[1]
    w_qk, w_v, w_r, w_ga, w_sq, w_sk, w_sv, w_gt = (
        w_in[:, a:b] for a, b in zip(bounds[:-1], bounds[1:]))
    w_cat = jnp.concatenate(
        [w_qk, w_v, w_r, w_sq, _duplicate_heads(w_sk, SWA_KV_HEADS),
         _duplicate_heads(w_sv, SWA_KV_HEADS), w_gt], axis=1).astype(BF16)
    widths = (2 * hk, hv, hv, nq, 2 * nkv, 2 * nkv, 2 * d)
    w_ga = jnp.pad(w_ga, ((0, 0), (0, LANES - rank))).astype(BF16)
    w_up = jnp.pad(p["w_gate_up"], ((0, LANES - rank), (0, 0)))
    qk, gv, gr, sq, skd, svd, gates, g = _in_proj(
        h2, p["norm_mix_gain"].reshape(1, d), w_cat, w_ga, w_up,
        p["b_gate_up"].reshape(1, hk), widths, tm)

    y_gla = _gla(qk, gv, gr, g, p["gla_out_gain"].reshape(1, GLA_DV), batch, seq, min(512, seq))

    one_hot = (np.arange(REL_BUCKETS)[:, None] == _t5_bucket_table().reshape(1, -1)).astype(np.float32)
    bias = jnp.dot(p["rel_bias"].astype(F32).T, one_hot, precision=lax.Precision.HIGHEST)
    bias = bias.reshape(SWA_Q_HEADS, SWA_WINDOW, 2 * SWA_WINDOW)
    window = _window_mask()
    no_prev = window & (np.arange(2 * SWA_WINDOW)[None, :] >= SWA_WINDOW)
    bias = jnp.where(np.stack([window, no_prev])[:, None], bias[None], -jnp.inf)
    bias = bias.reshape(2, SWA_Q_HEADS // 2, 2, SWA_WINDOW, 2 * SWA_WINDOW)
    bias = jnp.transpose(bias, (0, 1, 3, 2, 4)).reshape(2, SWA_KV_HEADS, -1, 4 * SWA_WINDOW)
    q_scale = jnp.tile(p["swa_q_gain"], SWA_Q_HEADS).reshape(1, nq) * (SWA_HEAD_DIM ** -0.5)
    k_scale = jnp.tile(p["swa_k_gain"], 2 * SWA_KV_HEADS).reshape(1, 2 * nkv)
    y_swa = _swa(sq, skd, svd, bias, q_scale, k_scale,
                 jnp.broadcast_to(p["swa_sinks"].reshape(SWA_Q_HEADS, 1, 1), (SWA_Q_HEADS, 1, LANES)),
                 batch, seq)

    sk = p["peer_sub_keys"].reshape(2 * PEER_HEADS, PEER_N_KEYS, PEER_HALF_DIM)
    h_mid, hnt, rank2, e2, count, coef = _merge_select(
        h2, y_gla, y_swa, gates, p["w_branch_gla"].astype(BF16),
        p["w_branch_swa"].astype(BF16), p["w_out"].astype(BF16),
        p["norm_ffn_gain"].reshape(1, d), p["peer_w_query"].astype(BF16), sk, min(256, t))
    dn = _pack_rows(p["peer_expert_down"], 1024)
    upt = _pack_rows(p["peer_expert_up"].T, 64)
    return _peer_dense(hnt, h_mid, dn, upt, rank2, e2, count, coef, tm, 2048, 2)


def kernel(x, norm_mix_gain, w_in, w_gate_up, b_gate_up, gla_out_gain, swa_q_gain,
           swa_k_gain, swa_sinks, rel_bias, w_branch_gla, w_branch_swa, w_out,
           norm_ffn_gain, peer_w_query, peer_sub_keys, peer_expert_down, peer_expert_up):
    batch, seq, d = x.shape
    h2 = x.reshape(batch * seq, d)
    for layer in range(w_in.shape[0]):
        p = dict(norm_mix_gain=norm_mix_gain[layer], w_in=w_in[layer],
                 w_gate_up=w_gate_up[layer], b_gate_up=b_gate_up[layer],
                 gla_out_gain=gla_out_gain[layer], swa_q_gain=swa_q_gain[layer],
                 swa_k_gain=swa_k_gain[layer], swa_sinks=swa_sinks[layer],
                 rel_bias=rel_bias, w_branch_gla=w_branch_gla[layer],
                 w_branch_swa=w_branch_swa[layer], w_out=w_out[layer],
                 norm_ffn_gain=norm_ffn_gain[layer], peer_w_query=peer_w_query[layer],
                 peer_sub_keys=peer_sub_keys[layer], peer_expert_down=peer_expert_down[layer],
                 peer_expert_up=peer_expert_up[layer])
        h2 = _layer(h2, batch, seq, p)
    return h2.reshape(batch, seq, d)
```

```python
import functools
import math

import numpy as np
import jax
import jax.numpy as jnp
from jax import lax
from jax.experimental import pallas as pl
from jax.experimental.pallas import tpu as pltpu

F32 = jnp.float32
BF16 = jnp.bfloat16

EPS = 1e-6

GLA_HEADS = 4
GLA_DK = 128
GLA_DV = 256
GLA_GATE_RANK = 16
GLA_GATE_NORMALIZER = 16.0
GLA_CHUNK = 64
GLA_SUB = 16

SWA_Q_HEADS = 16
SWA_KV_HEADS = 4
SWA_HEAD_DIM = 64
SWA_WINDOW = 128
REL_BUCKETS = 32
REL_MAX_DISTANCE = 128

PEER_HEADS = 8
PEER_N_KEYS = 128
PEER_HALF_DIM = 128
PEER_TOPK = 16

LANES = 128
VMEM_LIMIT = 56 * 1024 * 1024

NT_DIMS = (((1,), (1,)), ((), ()))
TN_DIMS = (((0,), (0,)), ((), ()))


def _cparams(semantics):
    return pltpu.CompilerParams(dimension_semantics=semantics,
                                vmem_limit_bytes=VMEM_LIMIT)


def _const_spec(shape):
    nd = len(shape)
    return pl.BlockSpec(shape, lambda *_: (0,) * nd)


def _split_bf16(x):
    hi = x.astype(BF16)
    lo = (x - hi.astype(F32)).astype(BF16)
    return hi, lo


def _pack_rows_kernel(x_ref, o_ref):
    o_ref[...] = pltpu.bitcast(x_ref[...].astype(BF16), jnp.uint32)


def _pack_rows(x, tr):
    r, c = x.shape
    return pl.pallas_call(
        _pack_rows_kernel,
        grid=(r // tr,),
        in_specs=[pl.BlockSpec((tr, c), lambda i: (i, 0))],
        out_specs=pl.BlockSpec((tr // 2, c), lambda i: (i, 0)),
        out_shape=jax.ShapeDtypeStruct((r // 2, c), jnp.uint32),
        compiler_params=_cparams(("parallel",)),
        name="pack_rows",
    )(x)


def _in_proj_kernel(x_ref, g_ref, w_ref, wga_ref, wup_ref, bup_ref, *out_refs):
    x = x_ref[...]
    ms = jnp.mean(x * x, axis=-1, keepdims=True)
    xn = (x * lax.rsqrt(ms + EPS) * g_ref[...]).astype(BF16)
    off = 0
    for o_ref in out_refs[:-1]:
        width = o_ref.shape[1]
        o_ref[...] = jnp.dot(xn, w_ref[:, off:off + width],
                             preferred_element_type=F32).astype(o_ref.dtype)
        off += width
    a_low = jnp.dot(xn, wga_ref[...], preferred_element_type=F32)
    pre = jnp.dot(a_low, wup_ref[...], preferred_element_type=F32,
                  precision=lax.Precision.HIGHEST) + bup_ref[...]
    out_refs[-1][...] = jax.nn.log_sigmoid(pre) * (1.0 / GLA_GATE_NORMALIZER)


def _in_proj(x2, gain, w_cat, w_ga, w_up, b_up, widths, tm):
    t, d = x2.shape
    nw = w_cat.shape[1]
    ng = w_up.shape[1]
    out_shape = [jax.ShapeDtypeStruct((t, w), BF16) for w in widths]
    out_shape.append(jax.ShapeDtypeStruct((t, ng), F32))
    out_specs = [pl.BlockSpec((tm, w), lambda i: (i, 0)) for w in widths + (ng,)]
    return pl.pallas_call(
        _in_proj_kernel,
        grid=(t // tm,),
        in_specs=[pl.BlockSpec((tm, d), lambda i: (i, 0)),
                  _const_spec((1, d)),
                  _const_spec((d, nw)),
                  _const_spec(w_ga.shape),
                  _const_spec(w_up.shape),
                  _const_spec((1, ng))],
        out_specs=out_specs,
        out_shape=out_shape,
        compiler_params=_cparams(("parallel",)),
        name="in_proj",
    )(x2, gain, w_cat, w_ga, w_up, b_up)


def _gla_chunk_head(q, k, v, g, st, tril):
    c_rows, sub = GLA_CHUNK, GLA_SUB
    row_s = lax.broadcasted_iota(jnp.int32, (c_rows, sub), 0)
    sub_j = lax.broadcasted_iota(jnp.int32, (sub, sub), 1)
    sub_r = lax.broadcasted_iota(jnp.int32, (sub, 1), 0)
    g_hi, g_lo = _split_bf16(g)
    b = (jnp.dot(tril, g_hi, preferred_element_type=F32)
         + jnp.dot(tril, g_lo, preferred_element_type=F32))
    o = lax.dot_general((q * jnp.exp(b)).astype(BF16), st.astype(BF16), NT_DIMS,
                        preferred_element_type=F32)
    for j in range(c_rows // sub):
        lo, hi = j * sub, (j + 1) * sub
        b_j, k_j, q_j = b[lo:hi, :], k[lo:hi, :], q[lo:hi, :]
        b_end = b[hi - 1:hi, :]
        diag = jnp.zeros((sub, sub), F32)
        for jj in range(sub):
            dec = jnp.exp(jnp.minimum(b_j - b_j[jj:jj + 1, :], 0.0))
            colv = jnp.sum(q_j * k_j[jj:jj + 1, :] * dec, axis=-1, keepdims=True)
            colv = jnp.where(sub_r >= jj, colv, 0.0)
            diag = jnp.where(sub_j == jj, colv, diag)
        pieces = []
        if lo > 0:
            pieces.append(jnp.zeros((lo, sub), F32))
        pieces.append(diag)
        if hi < c_rows:
            pieces.append(jnp.zeros((c_rows - hi, sub), F32))
        s_col = jnp.concatenate(pieces, axis=0)
        if hi < c_rows:
            k_t = (k_j * jnp.exp(b_end - b_j)).astype(BF16)
            q_t = (q * jnp.exp(jnp.minimum(b - b_end, 0.0))).astype(BF16)
            s_off = lax.dot_general(q_t, k_t, NT_DIMS, preferred_element_type=F32)
            s_col = s_col + jnp.where(row_s >= hi, s_off, 0.0)
        o = o + jnp.dot(s_col.astype(BF16), v[lo:hi, :], preferred_element_type=F32)
    b_last = b[c_rows - 1:c_rows, :]
    k_s = (k * jnp.exp(b_last - b)).astype(BF16)
    upd = lax.dot_general(v, k_s, TN_DIMS, preferred_element_type=F32)
    return o, st * jnp.exp(b_last) + upd


def _gla_kernel(qk_ref, v_ref, r_ref, g_ref, gain_ref, o_ref, state_ref, *, n_chunks):
    c_rows = GLA_CHUNK
    nh, dk, dv = GLA_HEADS, GLA_DK, GLA_DV

    @pl.when(pl.program_id(1) == 0)
    def _():
        state_ref[...] = jnp.zeros_like(state_ref)

    row = lax.broadcasted_iota(jnp.int32, (c_rows, c_rows), 0)
    col = lax.broadcasted_iota(jnp.int32, (c_rows, c_rows), 1)
    tril = (row >= col).astype(BF16)

    def chunk(c, carry):
        rows = pl.ds(pl.multiple_of(c * c_rows, c_rows), c_rows)
        for h in range(nh):
            q = qk_ref[rows, h * dk:(h + 1) * dk].astype(F32) * (dk ** -0.5)
            k = qk_ref[rows, (nh + h) * dk:(nh + h + 1) * dk].astype(F32)
            v = v_ref[rows, h * dv:(h + 1) * dv]
            g = g_ref[rows, h * dk:(h + 1) * dk]
            o, st = _gla_chunk_head(q, k, v, g, state_ref[h], tril)
            state_ref[h] = st
            ms = jnp.mean(o * o, axis=-1, keepdims=True)
            on = o * lax.rsqrt(ms + EPS) * gain_ref[...]
            r = r_ref[rows, h * dv:(h + 1) * dv].astype(F32)
            o_ref[rows, h * dv:(h + 1) * dv] = (on * (r * jax.nn.sigmoid(r))).astype(o_ref.dtype)
        return carry

    lax.fori_loop(0, n_chunks, chunk, 0)


def _gla(qk, gv, gr, g, out_gain, batch, seq, ts):
    nh, dk, dv = GLA_HEADS, GLA_DK, GLA_DV
    tok = lambda w: pl.BlockSpec((None, ts, w), lambda b, s: (b, s, 0))
    kern = functools.partial(_gla_kernel, n_chunks=ts // GLA_CHUNK)
    out = pl.pallas_call(
        kern,
        grid=(batch, seq // ts),
        in_specs=[tok(2 * nh * dk), tok(nh * dv), tok(nh * dv), tok(nh * dk),
                  _const_spec((1, dv))],
        out_specs=tok(nh * dv),
        out_shape=jax.ShapeDtypeStruct((batch, seq, nh * dv), BF16),
        scratch_shapes=[pltpu.VMEM((nh, dv, dk), F32)],
        compiler_params=_cparams(("parallel", "arbitrary")),
        name="gla",
    )(qk.reshape(batch, seq, -1), gv.reshape(batch, seq, -1), gr.reshape(batch, seq, -1),
      g.reshape(batch, seq, -1), out_gain)
    return out.reshape(batch * seq, nh * dv)


def _t5_bucket_table():
    w = SWA_WINDOW
    qi = np.arange(w)[:, None]
    kj = np.arange(2 * w)[None, :]
    rel = np.clip(qi + w - kj, 0, None)
    max_exact = REL_BUCKETS // 2
    rel_f = np.maximum(rel, 1).astype(np.float32)
    large = max_exact + (np.log(rel_f / np.float32(max_exact))
                         / np.float32(math.log(REL_MAX_DISTANCE / max_exact))
                         * np.float32(REL_BUCKETS - max_exact)).astype(np.int32)
    large = np.minimum(large, REL_BUCKETS - 1)
    return np.where(rel < max_exact, rel, large).astype(np.int32)


def _window_mask():
    w = SWA_WINDOW
    rel = np.arange(w)[:, None] + w - np.arange(2 * w)[None, :]
    return (rel >= 0) & (rel < w)


def _segment_matrix(width):
    seg = np.arange(width)[:, None] // SWA_HEAD_DIM
    return (seg == np.arange(LANES)[None, :]).astype(np.float32)


def _segment_rsqrt(x, e_ref, et_ref):
    hi, lo = _split_bf16(x * x)
    ss = (jnp.dot(hi, e_ref[...], preferred_element_type=F32)
          + jnp.dot(lo, e_ref[...], preferred_element_type=F32))
    hi, lo = _split_bf16(lax.rsqrt(ss * (1.0 / SWA_HEAD_DIM) + EPS))
    return (jnp.dot(hi, et_ref[...], preferred_element_type=F32)
            + jnp.dot(lo, et_ref[...], preferred_element_type=F32))


def _swa_kernel(q_ref, kp_ref, kc_ref, vp_ref, vc_ref, bias_ref, qs_ref, ks_ref, sink_ref,
                eq_ref, eqt_ref, ek_ref, ekt_ref, o_ref):
    w, hd = SWA_WINDOW, SWA_HEAD_DIM
    group = SWA_Q_HEADS // SWA_KV_HEADS
    n = pl.program_id(1)
    band = 2 * w
    n_pairs = SWA_Q_HEADS // 2
    first = (n == 0).astype(jnp.int32)
    low_k = lax.broadcasted_iota(jnp.int32, (band, LANES), 1) < hd
    low_q = lax.broadcasted_iota(jnp.int32, (w, LANES), 1) < hd

    q = q_ref[...].astype(F32)
    qn = (q * _segment_rsqrt(q, eq_ref, eqt_ref) * qs_ref[...]).astype(BF16)
    kd = jnp.concatenate([kp_ref[...], kc_ref[...]], axis=0).astype(F32)
    kn = (kd * _segment_rsqrt(kd, ek_ref, ekt_ref) * ks_ref[...]).astype(BF16)
    vd = jnp.concatenate([vp_ref[...], vc_ref[...]], axis=0)
    zero = jnp.zeros((band, LANES), BF16)
    ones_low = jnp.where(low_k, 1.0, 0.0).astype(BF16)
    ones_high = jnp.where(low_k, 0.0, 1.0).astype(BF16)
    k_both, v_both = [], []
    for g in range(SWA_KV_HEADS):
        kg = kn[:, g * LANES:(g + 1) * LANES]
        vg = vd[:, g * LANES:(g + 1) * LANES]
        k_both.append(jnp.concatenate([jnp.where(low_k, kg, zero), jnp.where(low_k, zero, kg)], axis=0))
        v_both.append(jnp.concatenate(
            [jnp.concatenate([jnp.where(low_k, vg, zero), ones_low], axis=1),
             jnp.concatenate([jnp.where(low_k, zero, vg), ones_high], axis=1)],
            axis=0))

    pairs_per_group = group // 2

    def logits_of(g):
        qs = jnp.concatenate([qn[:, p * LANES:(p + 1) * LANES]
                              for p in range(g * pairs_per_group, (g + 1) * pairs_per_group)], axis=0)
        lg = lax.dot_general(qs, k_both[g], NT_DIMS, preferred_element_type=F32)
        return lg + bias_ref[first, g]

    def probs_of(g, lg):
        es, sink_es = [], []
        for p in range(pairs_per_group):
            rows = slice(p * w, (p + 1) * w)
            e_p, s_p = [], []
            for half in range(2):
                hq = 2 * (g * pairs_per_group + p) + half
                sink = sink_ref[hq]
                lh = lg[rows, half * band:(half + 1) * band]
                m = jnp.maximum(jnp.broadcast_to(jnp.max(lh, axis=-1, keepdims=True), (w, LANES)),
                                sink)
                e_p.append(jnp.exp(lh - jnp.concatenate([m] * (band // LANES), axis=1)).astype(BF16))
                s_p.append(jnp.exp(sink - m))
            es.append(jnp.concatenate(e_p, axis=1))
            sink_es.append(jnp.where(low_q, s_p[0], s_p[1]))
        return jnp.concatenate(es, axis=0), jnp.concatenate(sink_es, axis=0)

    def out_of(g, e, sink_e):
        pv = jnp.dot(e, v_both[g], preferred_element_type=F32)
        return pv[:, :LANES] / (pv[:, LANES:] + sink_e)

    lead = 2
    n_groups = SWA_KV_HEADS
    logits, probs = {}, {}
    for step in range(n_groups + lead):
        if step < n_groups:
            logits[step] = logits_of(step)
        if 1 <= step <= n_groups:
            probs[step - 1] = probs_of(step - 1, logits.pop(step - 1))
        if step >= lead:
            g = step - lead
            res = out_of(g, *probs.pop(g)).astype(o_ref.dtype)
            for p in range(pairs_per_group):
                grp = g * pairs_per_group + p
                o_ref[:, grp * LANES:(grp + 1) * LANES] = res[p * w:(p + 1) * w, :]


def _swa(sq, skd, svd, bias, q_scale, k_scale, sinks, batch, seq):
    w, hd = SWA_WINDOW, SWA_HEAD_DIM
    hq, hkv = SWA_Q_HEADS, SWA_KV_HEADS
    nq, nkd = hq * hd, hkv * LANES
    eq = _segment_matrix(nq)
    ek = _segment_matrix(nkd)
    prev = lambda b, n: (b, jnp.maximum(n - 1, 0), 0)
    cur = lambda b, n: (b, n, 0)
    out = pl.pallas_call(
        _swa_kernel,
        grid=(batch, seq // w),
        in_specs=[
            pl.BlockSpec((None, w, nq), cur),
            pl.BlockSpec((None, w, nkd), prev),
            pl.BlockSpec((None, w, nkd), cur),
            pl.BlockSpec((None, w, nkd), prev),
            pl.BlockSpec((None, w, nkd), cur),
            _const_spec((2, hkv, (hq // hkv // 2) * w, 4 * w)),
            _const_spec((1, nq)),
            _const_spec((1, nkd)),
            _const_spec((hq, 1, LANES)),
            _const_spec(eq.shape), _const_spec(eq.T.shape),
            _const_spec(ek.shape), _const_spec(ek.T.shape),
        ],
        out_specs=pl.BlockSpec((None, w, nq), cur),
        out_shape=jax.ShapeDtypeStruct((batch, seq, nq), BF16),
        compiler_params=_cparams(("parallel", "parallel")),
        name="swa",
    )(sq.reshape(batch, seq, nq), *([skd.reshape(batch, seq, nkd)] * 2),
      *([svd.reshape(batch, seq, nkd)] * 2), bias, q_scale, k_scale, sinks,
      jnp.asarray(eq, BF16), jnp.asarray(eq.T, BF16), jnp.asarray(ek, BF16), jnp.asarray(ek.T, BF16))
    return out.reshape(batch * seq, nq)


def _top_values(s, k, with_rank=False):
    t = s.shape[1]
    row = lax.broadcasted_iota(jnp.int32, (k, t), 0)
    vals = jnp.zeros((k, t), F32)
    rank = jnp.full(s.shape, float(k), F32) if with_rank else None
    cur = s
    for r in range(k):
        m = jnp.max(cur, axis=0, keepdims=True)
        vals = jnp.where(row == r, m, vals)
        hit = cur == m
        if with_rank:
            rank = jnp.where(hit, float(r), rank)
        if r + 1 < k:
            cur = jnp.where(hit, -jnp.inf, cur)
    return vals, rank


def _candidate_sums(t1, t2, k):
    assert k == 16
    sub = 8
    row = lax.broadcasted_iota(jnp.int32, (sub, t1.shape[1]), 0)
    pieces = [t1[0:1, :] + t2]
    for a in range(1, sub):
        pieces.append(jnp.where(row < k // (a + 1), t1[a:a + 1, :] + t2[0:sub, :], -jnp.inf))
    pieces.append(t1[sub:k, :] + t2[0:1, :])
    return jnp.concatenate(pieces, axis=0)


def _select_head(s1, s2):
    kk = PEER_TOPK
    t1, _ = _top_values(s1, kk)
    t2, rank2 = _top_values(s2, kk, with_rank=True)
    cand = _candidate_sums(t1, t2, kk)
    tau = _top_values(cand, kk)[0][kk - 1:kk, :]
    m1 = t1[0:1, :]
    m2 = t2[0:1, :]
    z = jnp.sum(jnp.where(cand >= tau, jnp.exp(cand - (m1 + m2)), 0.0), axis=0, keepdims=True)
    count = jnp.zeros(s1.shape, F32)
    for b in range(kk):
        count = jnp.where(s1 + t2[b:b + 1, :] >= tau, float(b + 1), count)
    return (pltpu.bitcast(rank2.astype(BF16), jnp.uint32),
            pltpu.bitcast(jnp.exp(s2 - m2).astype(BF16), jnp.uint32),
            count, jnp.exp(s1 - m1) / z)


def _merge_select_kernel(x_ref, ya_ref, yb_ref, gt_ref, wa_ref, wb_ref, wo_ref, gain_ref,
                         wq_ref, sk_ref, h_ref, hnt_ref, rank_ref, e2_ref, n_ref, c_ref, sc_ref):
    d = x_ref.shape[1]

    @pl.when(pl.program_id(0) == 0)
    def _():
        sc_ref[...] = jnp.zeros_like(sc_ref)

    pa = jnp.dot(ya_ref[...], wa_ref[...], preferred_element_type=F32)
    pb = jnp.dot(yb_ref[...], wb_ref[...], preferred_element_type=F32)
    gates = gt_ref[...].astype(F32)
    merged = jax.nn.sigmoid(gates[:, :d]) * pa + jax.nn.sigmoid(gates[:, d:]) * pb
    h = x_ref[...] + jnp.dot(merged.astype(BF16), wo_ref[...], preferred_element_type=F32)
    h_ref[...] = h
    ms = jnp.mean(h * h, axis=-1, keepdims=True)
    hn_f32 = h * lax.rsqrt(ms + EPS) * gain_ref[...]
    hn = hn_f32.astype(BF16)
    hnt_ref[...] = pltpu.bitcast(hn_f32.T.astype(BF16), jnp.uint32)
    q = jnp.dot(hn, wq_ref[...], preferred_element_type=F32)
    hd = PEER_HALF_DIM
    for head in range(PEER_HEADS):
        rank_ref[head], e2_ref[head], n_ref[head], c_ref[head] = _select_head(
            sc_ref[2 * head], sc_ref[2 * head + 1])
        for hp in (2 * head, 2 * head + 1):
            sc_ref[hp] = lax.dot_general(sk_ref[hp], q[:, hp * hd:(hp + 1) * hd], NT_DIMS,
                                         preferred_element_type=F32,
                                         precision=lax.Precision.HIGHEST)


def _merge_select(x2, ya, yb, gates, wa, wb, wo, gain, wq, sk, tm):
    t, d = x2.shape
    nq = wq.shape[1]
    nhp, nk, hd = sk.shape
    nh = nhp // 2
    n_tiles = t // tm
    cur = lambda i: (jnp.minimum(i, n_tiles - 1), 0)
    prev = lambda i: (0, 0, jnp.maximum(i - 1, 0))
    tok = lambda w: pl.BlockSpec((tm, w), cur)
    return pl.pallas_call(
        _merge_select_kernel,
        grid=(n_tiles + 1,),
        in_specs=[tok(d), tok(ya.shape[1]), tok(yb.shape[1]), tok(2 * d),
                  _const_spec(wa.shape), _const_spec(wb.shape), _const_spec(wo.shape),
                  _const_spec((1, d)), _const_spec((d, nq)), _const_spec((nhp, nk, hd))],
        out_specs=[tok(d), pl.BlockSpec((d // 2, tm), lambda i: (0, jnp.minimum(i, n_tiles - 1))),
                   pl.BlockSpec((nh, nk // 2, tm), prev), pl.BlockSpec((nh, nk // 2, tm), prev),
                   pl.BlockSpec((nh, nk, tm), prev), pl.BlockSpec((nh, nk, tm), prev)],
        out_shape=[jax.ShapeDtypeStruct((t, d), F32),
                   jax.ShapeDtypeStruct((d // 2, t), jnp.uint32),
                   jax.ShapeDtypeStruct((nh, nk // 2, t), jnp.uint32),
                   jax.ShapeDtypeStruct((nh, nk // 2, t), jnp.uint32),
                   jax.ShapeDtypeStruct((nh, nk, t), F32),
                   jax.ShapeDtypeStruct((nh, nk, t), F32)],
        scratch_shapes=[pltpu.VMEM((nhp, nk, tm), F32)],
        compiler_params=_cparams(("arbitrary",)),
        name="merge_select",
    )(x2, ya, yb, gates, wa, wb, wo, gain, wq, sk)


def _gated_activation(rank_ref, e2_ref, n_ref, c_ref, act_ref, w_ref, row0, n_rows):
    nk = PEER_N_KEYS
    half = nk // 2
    pack = 16
    tm = act_ref.shape[1]
    for tc in range(tm // LANES):
        lanes = slice(tc * LANES, (tc + 1) * LANES)
        for hf in range(2):
            words = slice(hf * half // 2, (hf + 1) * half // 2)
            gates = [None] * n_rows
            for h in range(PEER_HEADS):
                rk = pltpu.bitcast(rank_ref[h, words, lanes], BF16)
                ev = pltpu.bitcast(e2_ref[h, words, lanes], BF16)
                for ii in range(n_rows):
                    i1 = row0 + ii
                    cnt = jnp.broadcast_to(n_ref[h, i1:i1 + 1, lanes], (pack, LANES)).astype(BF16)
                    cf = jnp.broadcast_to(c_ref[h, i1:i1 + 1, lanes], (pack, LANES)).astype(BF16)
                    cnt = jnp.concatenate([cnt] * (half // pack), axis=0)
                    cf = jnp.concatenate([cf] * (half // pack), axis=0)
                    term = jnp.where(rk < cnt, ev * cf, jnp.zeros_like(ev))
                    gates[ii] = term if gates[ii] is None else gates[ii] + term
            for ii in range(n_rows):
                rows = slice(ii * nk + hf * half, ii * nk + (hf + 1) * half)
                a = act_ref[rows, lanes].astype(BF16)
                gelu = 0.5 * a * (1.0 + lax.erf(a * (2.0 ** -0.5)))
                w_ref[rows, lanes] = gates[ii] * gelu


def _peer_dense_kernel(hnt_ref, h_ref, dn_ref, upt_ref, rank_ref, e2_ref, n_ref, c_ref,
                       o_ref, act_ref, w_ref, acc_ref):
    e = pl.program_id(1)
    n_split, ebs, _ = act_ref.shape
    rows = ebs // PEER_N_KEYS

    @pl.when(e == 0)
    def _():
        acc_ref[...] = jnp.zeros_like(acc_ref)

    hnt = pltpu.bitcast(hnt_ref[...], BF16)
    for s in range(n_split):
        dn = pltpu.bitcast(dn_ref[s * ebs // 2:(s + 1) * ebs // 2, :], BF16)
        act_ref[s] = jnp.dot(dn, hnt, preferred_element_type=F32)
    for s in range(n_split):
        _gated_activation(rank_ref, e2_ref, n_ref, c_ref, act_ref.at[s], w_ref.at[s],
                          s * rows, rows)
    w = w_ref[...].reshape(n_split * ebs, w_ref.shape[2])
    acc_ref[...] += jnp.dot(pltpu.bitcast(upt_ref[...], BF16), w,
                            preferred_element_type=F32)

    @pl.when(e == pl.num_programs(1) - 1)
    def _():
        o_ref[...] = h_ref[...] + acc_ref[...].T


def _peer_dense(hnt, h, dn, upt, rank2, e2, count, coef, tm, eb, n_split):
    t, d = h.shape
    ne = upt.shape[1]
    nk = PEER_N_KEYS
    nh = PEER_HEADS
    rows = eb // nk
    ebs = eb // n_split
    return pl.pallas_call(
        _peer_dense_kernel,
        grid=(t // tm, ne // eb),
        in_specs=[
            pl.BlockSpec((d // 2, tm), lambda i, e: (0, i)),
            pl.BlockSpec((tm, d), lambda i, e: (i, 0)),
            pl.BlockSpec((eb // 2, d), lambda i, e: (e, 0)),
            pl.BlockSpec((d // 2, eb), lambda i, e: (0, e)),
            pl.BlockSpec((nh, nk // 2, tm), lambda i, e: (0, 0, i)),
            pl.BlockSpec((nh, nk // 2, tm), lambda i, e: (0, 0, i)),
            pl.BlockSpec((nh, rows, tm), lambda i, e: (0, e, i)),
            pl.BlockSpec((nh, rows, tm), lambda i, e: (0, e, i)),
        ],
        out_specs=pl.BlockSpec((tm, d), lambda i, e: (i, 0)),
        out_shape=jax.ShapeDtypeStruct((t, d), F32),
        scratch_shapes=[pltpu.VMEM((n_split, ebs, tm), F32),
                        pltpu.VMEM((n_split, ebs, tm), BF16),
                        pltpu.VMEM((d, tm), F32)],
        compiler_params=_cparams(("parallel", "arbitrary")),
        name="peer_dense",
    )(hnt, h, dn, upt, rank2, e2, count, coef)


def _duplicate_heads(w, n_heads):
    d = w.shape[0]
    w3 = w.reshape(d, n_heads, -1)
    return jnp.concatenate([w3, w3], axis=2).reshape(d, -1)


def _layer(h2, batch, seq, p):
    t, d = h2.shape
    hk, hv = GLA_HEADS * GLA_DK, GLA_HEADS * GLA_DV
    nq, nkv = SWA_Q_HEADS * SWA_HEAD_DIM, SWA_KV_HEADS * SWA_HEAD_DIM
    rank = GLA_GATE_RANK
    tm = min(512, t)
    w_in = p["w_in"]
    bounds = np.cumsum([0, 2 * hk, hv, hv, rank, nq, nkv, nkv, 2 * d])
    assert bounds[-1] == w_in.shape[1]
    w_qk, w_v, w_r, w_ga, w_sq, w_sk, w_sv, w_gt = (
        w_in[:, a:b] for a, b in zip(bounds[:-1], bounds[1:]))
    w_cat = jnp.concatenate(
        [w_qk, w_v, w_r, w_sq, _duplicate_heads(w_sk, SWA_KV_HEADS),
         _duplicate_heads(w_sv, SWA_KV_HEADS), w_gt], axis=1).astype(BF16)
    widths = (2 * hk, hv, hv, nq, 2 * nkv, 2 * nkv, 2 * d)
    w_ga = jnp.pad(w_ga, ((0, 0), (0, LANES - rank))).astype(BF16)
    w_up = jnp.pad(p["w_gate_up"], ((0, LANES - rank), (0, 0)))
    qk, gv, gr, sq, skd, svd, gates, g = _in_proj(
        h2, p["norm_mix_gain"].reshape(1, d), w_cat, w_ga, w_up,
        p["b_gate_up"].reshape(1, hk), widths, tm)

    y_gla = _gla(qk, gv, gr, g, p["gla_out_gain"].reshape(1, GLA_DV), batch, seq, min(512, seq))

    one_hot = (np.arange(REL_BUCKETS)[:, None] == _t5_bucket_table().reshape(1, -1)).astype(np.float32)
    bias = jnp.dot(p["rel_bias"].astype(F32).T, one_hot, precision=lax.Precision.HIGHEST)
    bias = bias.reshape(SWA_Q_HEADS, SWA_WINDOW, 2 * SWA_WINDOW)
    window = _window_mask()
    no_prev = window & (np.arange(2 * SWA_WINDOW)[None, :] >= SWA_WINDOW)
    bias = jnp.where(np.stack([window, no_prev])[:, None], bias[None], -jnp.inf)
    bias = bias.reshape(2, SWA_Q_HEADS // 2, 2, SWA_WINDOW, 2 * SWA_WINDOW)
    bias = jnp.transpose(bias, (0, 1, 3, 2, 4)).reshape(2, SWA_KV_HEADS, -1, 4 * SWA_WINDOW)
    q_scale = jnp.tile(p["swa_q_gain"], SWA_Q_HEADS).reshape(1, nq) * (SWA_HEAD_DIM ** -0.5)
    k_scale = jnp.tile(p["swa_k_gain"], 2 * SWA_KV_HEADS).reshape(1, 2 * nkv)
    y_swa = _swa(sq, skd, svd, bias, q_scale, k_scale,
                 jnp.broadcast_to(p["swa_sinks"].reshape(SWA_Q_HEADS, 1, 1), (SWA_Q_HEADS, 1, LANES)),
                 batch, seq)

    sk = p["peer_sub_keys"].reshape(2 * PEER_HEADS, PEER_N_KEYS, PEER_HALF_DIM)
    h_mid, hnt, rank2, e2, count, coef = _merge_select(
        h2, y_gla, y_swa, gates, p["w_branch_gla"].astype(BF16),
        p["w_branch_swa"].astype(BF16), p["w_out"].astype(BF16),
        p["norm_ffn_gain"].reshape(1, d), p["peer_w_query"].astype(BF16), sk, min(256, t))
    dn = _pack_rows(p["peer_expert_down"], 1024)
    upt = _pack_rows(p["peer_expert_up"].T, 64)
    return _peer_dense(hnt, h_mid, dn, upt, rank2, e2, count, coef, tm, 2048, 2)


def kernel(x, norm_mix_gain, w_in, w_gate_up, b_gate_up, gla_out_gain, swa_q_gain,
           swa_k_gain, swa_sinks, rel_bias, w_branch_gla, w_branch_swa, w_out,
           norm_ffn_gain, peer_w_query, peer_sub_keys, peer_expert_down, peer_expert_up):
    batch, seq, d = x.shape
    h2 = x.reshape(batch * seq, d)
    for layer in range(w_in.shape[0]):
        p = dict(norm_mix_gain=norm_mix_gain[layer], w_in=w_in[layer],
                 w_gate_up=w_gate_up[layer], b_gate_up=b_gate_up[layer],
                 gla_out_gain=gla_out_gain[layer], swa_q_gain=swa_q_gain[layer],
                 swa_k_gain=swa_k_gain[layer], swa_sinks=swa_sinks[layer],
                 rel_bias=rel_bias, w_branch_gla=w_branch_gla[layer],
                 w_branch_swa=w_branch_swa[layer], w_out=w_out[layer],
                 norm_ffn_gain=norm_ffn_gain[layer], peer_w_query=peer_w_query[layer],
                 peer_sub_keys=peer_sub_keys[layer], peer_expert_down=peer_expert_down[layer],
                 peer_expert_up=peer_expert_up[layer])
        h2 = _layer(h2, batch, seq, p)
    return h2.reshape(batch, seq, d)
```

```python
import functools
import math

import numpy as np
import jax
import jax.numpy as jnp
from jax import lax
from jax.experimental import pallas as pl
from jax.experimental.pallas import tpu as pltpu

F32 = jnp.float32
BF16 = jnp.bfloat16

EPS = 1e-6

GLA_HEADS = 4
GLA_DK = 128
GLA_DV = 256
GLA_GATE_RANK = 16
GLA_GATE_NORMALIZER = 16.0
GLA_CHUNK = 64
GLA_SUB = 16

SWA_Q_HEADS = 16
SWA_KV_HEADS = 4
SWA_HEAD_DIM = 64
SWA_WINDOW = 128
REL_BUCKETS = 32
REL_MAX_DISTANCE = 128

PEER_HEADS = 8
PEER_N_KEYS = 128
PEER_HALF_DIM = 128
PEER_TOPK = 16

LANES = 128
SUBLANES = 8
BF16_ROWS = 16
VMEM_LIMIT = 56 * 1024 * 1024

TOKEN_TILE = 512
SELECT_TILE = 256
GLA_SEQ_TILE = 512
EXPERT_BLOCK = 2048
EXPERT_SPLIT = 2
PACK_TILE = 1024

NT_DIMS = (((1,), (1,)), ((), ()))
TN_DIMS = (((0,), (0,)), ((), ()))


def _cparams(semantics):
    return pltpu.CompilerParams(dimension_semantics=semantics,
                                vmem_limit_bytes=VMEM_LIMIT)


def _const_spec(shape):
    nd = len(shape)
    return pl.BlockSpec(shape, lambda *_: (0,) * nd)


def _split_bf16(x):
    hi = x.astype(BF16)
    lo = (x - hi.astype(F32)).astype(BF16)
    return hi, lo


def _pack_rows_kernel(x_ref, o_ref):
    o_ref[...] = pltpu.bitcast(x_ref[...].astype(BF16), jnp.uint32)


def _pack_transposed_kernel(x_ref, o_ref):
    o_ref[...] = pltpu.bitcast(x_ref[...].T.astype(BF16), jnp.uint32)


def _pack_transposed(x, tr):
    r, c = x.shape
    return pl.pallas_call(
        _pack_transposed_kernel,
        grid=(r // tr,),
        in_specs=[pl.BlockSpec((tr, c), lambda i: (i, 0))],
        out_specs=pl.BlockSpec((c // 2, tr), lambda i: (0, i)),
        out_shape=jax.ShapeDtypeStruct((c // 2, r), jnp.uint32),
        compiler_params=_cparams(("parallel",)),
        name="pack_transposed",
    )(x)


def _pack_rows(x, tr):
    r, c = x.shape
    return pl.pallas_call(
        _pack_rows_kernel,
        grid=(r // tr,),
        in_specs=[pl.BlockSpec((tr, c), lambda i: (i, 0))],
        out_specs=pl.BlockSpec((tr // 2, c), lambda i: (i, 0)),
        out_shape=jax.ShapeDtypeStruct((r // 2, c), jnp.uint32),
        compiler_params=_cparams(("parallel",)),
        name="pack_rows",
    )(x)


def _in_proj_kernel(x_ref, g_ref, w_ref, wga_ref, wup_ref, bup_ref, *out_refs):
    x = x_ref[...]
    ms = jnp.mean(x * x, axis=-1, keepdims=True)
    xn = (x * lax.rsqrt(ms + EPS) * g_ref[...]).astype(BF16)
    off = 0
    for o_ref in out_refs[:-1]:
        width = o_ref.shape[1]
        o_ref[...] = jnp.dot(xn, w_ref[:, off:off + width],
                             preferred_element_type=F32).astype(o_ref.dtype)
        off += width
    a_low = jnp.dot(xn, wga_ref[...], preferred_element_type=F32)
    pre = jnp.dot(a_low, wup_ref[...], preferred_element_type=F32,
                  precision=lax.Precision.HIGHEST) + bup_ref[...]
    out_refs[-1][...] = jax.nn.log_sigmoid(pre) * (1.0 / GLA_GATE_NORMALIZER)


def _in_proj(x2, gain, w_cat, w_ga, w_up, b_up, widths, tm):
    t, d = x2.shape
    nw = w_cat.shape[1]
    ng = w_up.shape[1]
    out_shape = [jax.ShapeDtypeStruct((t, w), BF16) for w in widths]
    out_shape.append(jax.ShapeDtypeStruct((t, ng), F32))
    out_specs = [pl.BlockSpec((tm, w), lambda i: (i, 0)) for w in widths + (ng,)]
    return pl.pallas_call(
        _in_proj_kernel,
        grid=(t // tm,),
        in_specs=[pl.BlockSpec((tm, d), lambda i: (i, 0)),
                  _const_spec((1, d)),
                  _const_spec((d, nw)),
                  _const_spec(w_ga.shape),
                  _const_spec(w_up.shape),
                  _const_spec((1, ng))],
        out_specs=out_specs,
        out_shape=out_shape,
        compiler_params=_cparams(("parallel",)),
        name="in_proj",
    )(x2, gain, w_cat, w_ga, w_up, b_up)


def _gla_chunk_head(q, k, v, g, st, tril):
    c_rows, sub = GLA_CHUNK, GLA_SUB
    row_s = lax.broadcasted_iota(jnp.int32, (c_rows, sub), 0)
    sub_j = lax.broadcasted_iota(jnp.int32, (sub, sub), 1)
    sub_r = lax.broadcasted_iota(jnp.int32, (sub, 1), 0)
    g_hi, g_lo = _split_bf16(g)
    b = (jnp.dot(tril, g_hi, preferred_element_type=F32)
         + jnp.dot(tril, g_lo, preferred_element_type=F32))
    o = lax.dot_general((q * jnp.exp(b)).astype(BF16), st.astype(BF16), NT_DIMS,
                        preferred_element_type=F32)
    for j in range(c_rows // sub):
        lo, hi = j * sub, (j + 1) * sub
        b_j, k_j, q_j = b[lo:hi, :], k[lo:hi, :], q[lo:hi, :]
        b_end = b[hi - 1:hi, :]
        diag = jnp.zeros((sub, sub), F32)
        for jj in range(sub):
            dec = jnp.exp(jnp.minimum(b_j - b_j[jj:jj + 1, :], 0.0))
            colv = jnp.sum(q_j * k_j[jj:jj + 1, :] * dec, axis=-1, keepdims=True)
            colv = jnp.where(sub_r >= jj, colv, 0.0)
            diag = jnp.where(sub_j == jj, colv, diag)
        pieces = []
        if lo > 0:
            pieces.append(jnp.zeros((lo, sub), F32))
        pieces.append(diag)
        if hi < c_rows:
            pieces.append(jnp.zeros((c_rows - hi, sub), F32))
        s_col = jnp.concatenate(pieces, axis=0)
        if hi < c_rows:
            k_t = (k_j * jnp.exp(b_end - b_j)).astype(BF16)
            q_t = (q * jnp.exp(jnp.minimum(b - b_end, 0.0))).astype(BF16)
            s_off = lax.dot_general(q_t, k_t, NT_DIMS, preferred_element_type=F32)
            s_col = s_col + jnp.where(row_s >= hi, s_off, 0.0)
        o = o + jnp.dot(s_col.astype(BF16), v[lo:hi, :], preferred_element_type=F32)
    b_last = b[c_rows - 1:c_rows, :]
    k_s = (k * jnp.exp(b_last - b)).astype(BF16)
    upd = lax.dot_general(v, k_s, TN_DIMS, preferred_element_type=F32)
    return o, st * jnp.exp(b_last) + upd


def _gla_kernel(qk_ref, v_ref, r_ref, g_ref, gain_ref, o_ref, state_ref, *, n_chunks):
    c_rows = GLA_CHUNK
    nh, dk, dv = GLA_HEADS, GLA_DK, GLA_DV

    @pl.when(pl.program_id(1) == 0)
    def _():
        state_ref[...] = jnp.zeros_like(state_ref)

    row = lax.broadcasted_iota(jnp.int32, (c_rows, c_rows), 0)
    col = lax.broadcasted_iota(jnp.int32, (c_rows, c_rows), 1)
    tril = (row >= col).astype(BF16)

    def chunk(c, carry):
        rows = pl.ds(pl.multiple_of(c * c_rows, c_rows), c_rows)
        for h in range(nh):
            q = qk_ref[rows, h * dk:(h + 1) * dk].astype(F32) * (dk ** -0.5)
            k = qk_ref[rows, (nh + h) * dk:(nh + h + 1) * dk].astype(F32)
            v = v_ref[rows, h * dv:(h + 1) * dv]
            g = g_ref[rows, h * dk:(h + 1) * dk]
            o, st = _gla_chunk_head(q, k, v, g, state_ref[h], tril)
            state_ref[h] = st
            ms = jnp.mean(o * o, axis=-1, keepdims=True)
            on = o * lax.rsqrt(ms + EPS) * gain_ref[...]
            r = r_ref[rows, h * dv:(h + 1) * dv].astype(F32)
            o_ref[rows, h * dv:(h + 1) * dv] = (on * (r * jax.nn.sigmoid(r))).astype(o_ref.dtype)
        return carry

    lax.fori_loop(0, n_chunks, chunk, 0)


def _gla(qk, gv, gr, g, out_gain, batch, seq, ts):
    nh, dk, dv = GLA_HEADS, GLA_DK, GLA_DV
    tok = lambda w: pl.BlockSpec((None, ts, w), lambda b, s: (b, s, 0))
    kern = functools.partial(_gla_kernel, n_chunks=ts // GLA_CHUNK)
    out = pl.pallas_call(
        kern,
        grid=(batch, seq // ts),
        in_specs=[tok(2 * nh * dk), tok(nh * dv), tok(nh * dv), tok(nh * dk),
                  _const_spec((1, dv))],
        out_specs=tok(nh * dv),
        out_shape=jax.ShapeDtypeStruct((batch, seq, nh * dv), BF16),
        scratch_shapes=[pltpu.VMEM((nh, dv, dk), F32)],
        compiler_params=_cparams(("parallel", "arbitrary")),
        name="gla",
    )(qk.reshape(batch, seq, -1), gv.reshape(batch, seq, -1), gr.reshape(batch, seq, -1),
      g.reshape(batch, seq, -1), out_gain)
    return out.reshape(batch * seq, nh * dv)


def _t5_bucket_table():
    w = SWA_WINDOW
    qi = np.arange(w)[:, None]
    kj = np.arange(2 * w)[None, :]
    rel = np.clip(qi + w - kj, 0, None)
    max_exact = REL_BUCKETS // 2
    rel_f = np.maximum(rel, 1).astype(np.float32)
    large = max_exact + (np.log(rel_f / np.float32(max_exact))
                         / np.float32(math.log(REL_MAX_DISTANCE / max_exact))
                         * np.float32(REL_BUCKETS - max_exact)).astype(np.int32)
    large = np.minimum(large, REL_BUCKETS - 1)
    return np.where(rel < max_exact, rel, large).astype(np.int32)


def _window_mask():
    w = SWA_WINDOW
    rel = np.arange(w)[:, None] + w - np.arange(2 * w)[None, :]
    return (rel >= 0) & (rel < w)


def _segment_matrix(width):
    seg = np.arange(width)[:, None] // SWA_HEAD_DIM
    return (seg == np.arange(LANES)[None, :]).astype(np.float32)


def _segment_rsqrt(x, e_ref, et_ref):
    hi, lo = _split_bf16(x * x)
    ss = (jnp.dot(hi, e_ref[...], preferred_element_type=F32)
          + jnp.dot(lo, e_ref[...], preferred_element_type=F32))
    hi, lo = _split_bf16(lax.rsqrt(ss * (1.0 / SWA_HEAD_DIM) + EPS))
    return (jnp.dot(hi, et_ref[...], preferred_element_type=F32)
            + jnp.dot(lo, et_ref[...], preferred_element_type=F32))


def _swa_kernel(q_ref, kp_ref, kc_ref, vp_ref, vc_ref, bias_ref, qs_ref, ks_ref, sink_ref,
                eq_ref, eqt_ref, ek_ref, ekt_ref, o_ref):
    w, hd = SWA_WINDOW, SWA_HEAD_DIM
    group = SWA_Q_HEADS // SWA_KV_HEADS
    n = pl.program_id(1)
    band = 2 * w
    first = (n == 0).astype(jnp.int32)
    low_k = lax.broadcasted_iota(jnp.int32, (band, LANES), 1) < hd
    low_q = lax.broadcasted_iota(jnp.int32, (w, LANES), 1) < hd

    q = q_ref[...].astype(F32)
    qn = (q * _segment_rsqrt(q, eq_ref, eqt_ref) * qs_ref[...]).astype(BF16)
    kd = jnp.concatenate([kp_ref[...], kc_ref[...]], axis=0).astype(F32)
    kn = (kd * _segment_rsqrt(kd, ek_ref, ekt_ref) * ks_ref[...]).astype(BF16)
    vd = jnp.concatenate([vp_ref[...], vc_ref[...]], axis=0)
    zero = jnp.zeros((band, LANES), BF16)
    ones_low = jnp.where(low_k, 1.0, 0.0).astype(BF16)
    ones_high = jnp.where(low_k, 0.0, 1.0).astype(BF16)
    k_both, v_both = [], []
    for g in range(SWA_KV_HEADS):
        kg = kn[:, g * LANES:(g + 1) * LANES]
        vg = vd[:, g * LANES:(g + 1) * LANES]
        k_both.append(jnp.concatenate([jnp.where(low_k, kg, zero), jnp.where(low_k, zero, kg)], axis=0))
        v_both.append(jnp.concatenate(
            [jnp.concatenate([jnp.where(low_k, vg, zero), ones_low], axis=1),
             jnp.concatenate([jnp.where(low_k, zero, vg), ones_high], axis=1)],
            axis=0))

    pairs_per_group = group // 2

    def logits_of(g):
        qs = jnp.concatenate([qn[:, p * LANES:(p + 1) * LANES]
                              for p in range(g * pairs_per_group, (g + 1) * pairs_per_group)], axis=0)
        lg = lax.dot_general(qs, k_both[g], NT_DIMS, preferred_element_type=F32)
        return lg + bias_ref[first, g]

    def probs_of(g, lg):
        es, sink_es = [], []
        for p in range(pairs_per_group):
            rows = slice(p * w, (p + 1) * w)
            e_p, s_p = [], []
            for half in range(2):
                hq = 2 * (g * pairs_per_group + p) + half
                sink = sink_ref[hq]
                lh = lg[rows, half * band:(half + 1) * band]
                m = jnp.maximum(jnp.broadcast_to(jnp.max(lh, axis=-1, keepdims=True), (w, LANES)),
                                sink)
                e_p.append(jnp.exp(lh - jnp.concatenate([m] * (band // LANES), axis=1)).astype(BF16))
                s_p.append(jnp.exp(sink - m))
            es.append(jnp.concatenate(e_p, axis=1))
            sink_es.append(jnp.where(low_q, s_p[0], s_p[1]))
        return jnp.concatenate(es, axis=0), jnp.concatenate(sink_es, axis=0)

    def out_of(g, e, sink_e):
        pv = jnp.dot(e, v_both[g], preferred_element_type=F32)
        return pv[:, :LANES] / (pv[:, LANES:] + sink_e)

    lead = 2
    n_groups = SWA_KV_HEADS
    logits, probs = {}, {}
    for step in range(n_groups + lead):
        if step < n_groups:
            logits[step] = logits_of(step)
        if 1 <= step <= n_groups:
            probs[step - 1] = probs_of(step - 1, logits.pop(step - 1))
        if step >= lead:
            g = step - lead
            res = out_of(g, *probs.pop(g)).astype(o_ref.dtype)
            for p in range(pairs_per_group):
                grp = g * pairs_per_group + p
                o_ref[:, grp * LANES:(grp + 1) * LANES] = res[p * w:(p + 1) * w, :]


def _swa(sq, skd, svd, bias, q_scale, k_scale, sinks, batch, seq):
    w, hd = SWA_WINDOW, SWA_HEAD_DIM
    hq, hkv = SWA_Q_HEADS, SWA_KV_HEADS
    nq, nkd = hq * hd, hkv * LANES
    eq = _segment_matrix(nq)
    ek = _segment_matrix(nkd)
    prev = lambda b, n: (b, jnp.maximum(n - 1, 0), 0)
    cur = lambda b, n: (b, n, 0)
    out = pl.pallas_call(
        _swa_kernel,
        grid=(batch, seq // w),
        in_specs=[
            pl.BlockSpec((None, w, nq), cur),
            pl.BlockSpec((None, w, nkd), prev),
            pl.BlockSpec((None, w, nkd), cur),
            pl.BlockSpec((None, w, nkd), prev),
            pl.BlockSpec((None, w, nkd), cur),
            _const_spec((2, hkv, (hq // hkv // 2) * w, 4 * w)),
            _const_spec((1, nq)),
            _const_spec((1, nkd)),
            _const_spec((hq, 1, LANES)),
            _const_spec(eq.shape), _const_spec(eq.T.shape),
            _const_spec(ek.shape), _const_spec(ek.T.shape),
        ],
        out_specs=pl.BlockSpec((None, w, nq), cur),
        out_shape=jax.ShapeDtypeStruct((batch, seq, nq), BF16),
        compiler_params=_cparams(("parallel", "parallel")),
        name="swa",
    )(sq.reshape(batch, seq, nq), *([skd.reshape(batch, seq, nkd)] * 2),
      *([svd.reshape(batch, seq, nkd)] * 2), bias, q_scale, k_scale, sinks,
      jnp.asarray(eq, BF16), jnp.asarray(eq.T, BF16), jnp.asarray(ek, BF16), jnp.asarray(ek.T, BF16))
    return out.reshape(batch * seq, nq)


def _top_values(s, k, with_rank=False):
    t = s.shape[1]
    row = lax.broadcasted_iota(jnp.int32, (k, t), 0)
    vals = jnp.zeros((k, t), F32)
    rank = jnp.full(s.shape, float(k), F32) if with_rank else None
    cur = s
    for r in range(k):
        m = jnp.max(cur, axis=0, keepdims=True)
        vals = jnp.where(row == r, m, vals)
        hit = cur == m
        if with_rank:
            rank = jnp.where(hit, float(r), rank)
        if r + 1 < k:
            cur = jnp.where(hit, -jnp.inf, cur)
    return vals, rank


def _candidate_sums(t1, t2, k):
    assert k == 16
    sub = SUBLANES
    row = lax.broadcasted_iota(jnp.int32, (sub, t1.shape[1]), 0)
    pieces = [t1[0:1, :] + t2]
    for a in range(1, sub):
        pieces.append(jnp.where(row < k // (a + 1), t1[a:a + 1, :] + t2[0:sub, :], -jnp.inf))
    pieces.append(t1[sub:k, :] + t2[0:1, :])
    return jnp.concatenate(pieces, axis=0)


def _select_head(s1, s2):
    kk = PEER_TOPK
    t1, _ = _top_values(s1, kk)
    t2, rank2 = _top_values(s2, kk, with_rank=True)
    cand = _candidate_sums(t1, t2, kk)
    tau = _top_values(cand, kk)[0][kk - 1:kk, :]
    m1 = t1[0:1, :]
    m2 = t2[0:1, :]
    z = jnp.sum(jnp.where(cand >= tau, jnp.exp(cand - (m1 + m2)), 0.0), axis=0, keepdims=True)
    count = jnp.zeros(s1.shape, F32)
    for b in range(kk):
        count = jnp.where(s1 + t2[b:b + 1, :] >= tau, float(b + 1), count)
    return (pltpu.bitcast(rank2.astype(BF16), jnp.uint32),
            pltpu.bitcast(jnp.exp(s2 - m2).astype(BF16), jnp.uint32),
            count, jnp.exp(s1 - m1) / z)


def _merge_select_kernel(x_ref, ya_ref, yb_ref, gt_ref, wa_ref, wb_ref, wo_ref, gain_ref,
                         wq_ref, sk_ref, h_ref, hnt_ref, rank_ref, e2_ref, n_ref, c_ref, sc_ref):
    d = x_ref.shape[1]

    @pl.when(pl.program_id(0) == 0)
    def _():
        sc_ref[...] = jnp.zeros_like(sc_ref)

    pa = jnp.dot(ya_ref[...], wa_ref[...], preferred_element_type=F32)
    pb = jnp.dot(yb_ref[...], wb_ref[...], preferred_element_type=F32)
    gates = gt_ref[...].astype(F32)
    merged = jax.nn.sigmoid(gates[:, :d]) * pa + jax.nn.sigmoid(gates[:, d:]) * pb
    h = x_ref[...] + jnp.dot(merged.astype(BF16), wo_ref[...], preferred_element_type=F32)
    h_ref[...] = h
    ms = jnp.mean(h * h, axis=-1, keepdims=True)
    hn_f32 = h * lax.rsqrt(ms + EPS) * gain_ref[...]
    hn = hn_f32.astype(BF16)
    hnt_ref[...] = pltpu.bitcast(hn_f32.T.astype(BF16), jnp.uint32)
    q = jnp.dot(hn, wq_ref[...], preferred_element_type=F32)
    hd = PEER_HALF_DIM
    for head in range(PEER_HEADS):
        rank_ref[head], e2_ref[head], n_ref[head], c_ref[head] = _select_head(
            sc_ref[2 * head], sc_ref[2 * head + 1])
        for hp in (2 * head, 2 * head + 1):
            sc_ref[hp] = lax.dot_general(sk_ref[hp], q[:, hp * hd:(hp + 1) * hd], NT_DIMS,
                                         preferred_element_type=F32,
                                         precision=lax.Precision.HIGHEST)


def _merge_select(x2, ya, yb, gates, wa, wb, wo, gain, wq, sk, tm):
    t, d = x2.shape
    nq = wq.shape[1]
    nhp, nk, hd = sk.shape
    nh = nhp // 2
    n_tiles = t // tm
    cur = lambda i: (jnp.minimum(i, n_tiles - 1), 0)
    prev = lambda i: (0, 0, jnp.maximum(i - 1, 0))
    tok = lambda w: pl.BlockSpec((tm, w), cur)
    return pl.pallas_call(
        _merge_select_kernel,
        grid=(n_tiles + 1,),
        in_specs=[tok(d), tok(ya.shape[1]), tok(yb.shape[1]), tok(2 * d),
                  _const_spec(wa.shape), _const_spec(wb.shape), _const_spec(wo.shape),
                  _const_spec((1, d)), _const_spec((d, nq)), _const_spec((nhp, nk, hd))],
        out_specs=[tok(d), pl.BlockSpec((d // 2, tm), lambda i: (0, jnp.minimum(i, n_tiles - 1))),
                   pl.BlockSpec((nh, nk // 2, tm), prev), pl.BlockSpec((nh, nk // 2, tm), prev),
                   pl.BlockSpec((nh, nk, tm), prev), pl.BlockSpec((nh, nk, tm), prev)],
        out_shape=[jax.ShapeDtypeStruct((t, d), F32),
                   jax.ShapeDtypeStruct((d // 2, t), jnp.uint32),
                   jax.ShapeDtypeStruct((nh, nk // 2, t), jnp.uint32),
                   jax.ShapeDtypeStruct((nh, nk // 2, t), jnp.uint32),
                   jax.ShapeDtypeStruct((nh, nk, t), F32),
                   jax.ShapeDtypeStruct((nh, nk, t), F32)],
        scratch_shapes=[pltpu.VMEM((nhp, nk, tm), F32)],
        compiler_params=_cparams(("arbitrary",)),
        name="merge_select",
    )(x2, ya, yb, gates, wa, wb, wo, gain, wq, sk)


def _gated_activation(rank_ref, e2_ref, n_ref, c_ref, act_ref, w_ref, row0, n_rows):
    nk = PEER_N_KEYS
    half = nk // 2
    pack = BF16_ROWS
    tm = act_ref.shape[1]
    for tc in range(tm // LANES):
        lanes = slice(tc * LANES, (tc + 1) * LANES)
        for hf in range(2):
            words = slice(hf * half // 2, (hf + 1) * half // 2)
            gates = [None] * n_rows
            for h in range(PEER_HEADS):
                rk = pltpu.bitcast(rank_ref[h, words, lanes], BF16)
                ev = pltpu.bitcast(e2_ref[h, words, lanes], BF16)
                for ii in range(n_rows):
                    i1 = row0 + ii
                    cnt = jnp.broadcast_to(n_ref[h, i1:i1 + 1, lanes], (pack, LANES)).astype(BF16)
                    cf = jnp.broadcast_to(c_ref[h, i1:i1 + 1, lanes], (pack, LANES)).astype(BF16)
                    cnt = jnp.concatenate([cnt] * (half // pack), axis=0)
                    cf = jnp.concatenate([cf] * (half // pack), axis=0)
                    term = jnp.where(rk < cnt, ev * cf, jnp.zeros_like(ev))
                    gates[ii] = term if gates[ii] is None else gates[ii] + term
            for ii in range(n_rows):
                rows = slice(ii * nk + hf * half, ii * nk + (hf + 1) * half)
                a = act_ref[rows, lanes]
                gelu = 0.5 * a * (1.0 + lax.erf(a * (2.0 ** -0.5)))
                w_ref[rows, lanes] = gates[ii] * gelu.astype(BF16)


def _peer_dense_kernel(hnt_ref, h_ref, dn_ref, upt_ref, rank_ref, e2_ref, n_ref, c_ref,
                       o_ref, act_ref, w_ref, acc_ref):
    e = pl.program_id(1)
    n_split, ebs, _ = act_ref.shape
    rows = ebs // PEER_N_KEYS

    @pl.when(e == 0)
    def _():
        acc_ref[...] = jnp.zeros_like(acc_ref)

    hnt = pltpu.bitcast(hnt_ref[...], BF16)
    for s in range(n_split):
        dn = pltpu.bitcast(dn_ref[s * ebs // 2:(s + 1) * ebs // 2, :], BF16)
        act_ref[s] = jnp.dot(dn, hnt, preferred_element_type=F32)
    for s in range(n_split):
        _gated_activation(rank_ref, e2_ref, n_ref, c_ref, act_ref.at[s], w_ref.at[s],
                          s * rows, rows)
    w = w_ref[...].reshape(n_split * ebs, w_ref.shape[2])
    acc_ref[...] += jnp.dot(pltpu.bitcast(upt_ref[...], BF16), w,
                            preferred_element_type=F32)

    @pl.when(e == pl.num_programs(1) - 1)
    def _():
        o_ref[...] = h_ref[...] + acc_ref[...].T


def _peer_dense(hnt, h, dn, upt, rank2, e2, count, coef, tm, eb, n_split):
    t, d = h.shape
    ne = upt.shape[1]
    nk = PEER_N_KEYS
    nh = PEER_HEADS
    rows = eb // nk
    ebs = eb // n_split
    return pl.pallas_call(
        _peer_dense_kernel,
        grid=(t // tm, ne // eb),
        in_specs=[
            pl.BlockSpec((d // 2, tm), lambda i, e: (0, i)),
            pl.BlockSpec((tm, d), lambda i, e: (i, 0)),
            pl.BlockSpec((eb // 2, d), lambda i, e: (e, 0)),
            pl.BlockSpec((d // 2, eb), lambda i, e: (0, e)),
            pl.BlockSpec((nh, nk // 2, tm), lambda i, e: (0, 0, i)),
            pl.BlockSpec((nh, nk // 2, tm), lambda i, e: (0, 0, i)),
            pl.BlockSpec((nh, rows, tm), lambda i, e: (0, e, i)),
            pl.BlockSpec((nh, rows, tm), lambda i, e: (0, e, i)),
        ],
        out_specs=pl.BlockSpec((tm, d), lambda i, e: (i, 0)),
        out_shape=jax.ShapeDtypeStruct((t, d), F32),
        scratch_shapes=[pltpu.VMEM((n_split, ebs, tm), F32),
                        pltpu.VMEM((n_split, ebs, tm), BF16),
                        pltpu.VMEM((d, tm), F32)],
        compiler_params=_cparams(("parallel", "arbitrary")),
        name="peer_dense",
    )(hnt, h, dn, upt, rank2, e2, count, coef)


def _duplicate_heads(w, n_heads):
    d = w.shape[0]
    w3 = w.reshape(d, n_heads, -1)
    return jnp.concatenate([w3, w3], axis=2).reshape(d, -1)


def _layer(h2, batch, seq, p):
    t, d = h2.shape
    hk, hv = GLA_HEADS * GLA_DK, GLA_HEADS * GLA_DV
    nq, nkv = SWA_Q_HEADS * SWA_HEAD_DIM, SWA_KV_HEADS * SWA_HEAD_DIM
    rank = GLA_GATE_RANK
    tm = min(TOKEN_TILE, t)
    w_in = p["w_in"]
    bounds = np.cumsum([0, 2 * hk, hv, hv, rank, nq, nkv, nkv, 2 * d])
    assert bounds[-1] == w_in.shape[1]
    w_qk, w_v, w_r, w_ga, w_sq, w_sk, w_sv, w_gt = (
        w_in[:, a:b] for a, b in zip(bounds[:-1], bounds[1:]))
    w_cat = jnp.concatenate(
        [w_qk, w_v, w_r, w_sq, _duplicate_heads(w_sk, SWA_KV_HEADS),
         _duplicate_heads(w_sv, SWA_KV_HEADS), w_gt], axis=1).astype(BF16)
    widths = (2 * hk, hv, hv, nq, 2 * nkv, 2 * nkv, 2 * d)
    w_ga = jnp.pad(w_ga, ((0, 0), (0, LANES - rank))).astype(BF16)
    w_up = jnp.pad(p["w_gate_up"], ((0, LANES - rank), (0, 0)))
    qk, gv, gr, sq, skd, svd, gates, g = _in_proj(
        h2, p["norm_mix_gain"].reshape(1, d), w_cat, w_ga, w_up,
        p["b_gate_up"].reshape(1, hk), widths, tm)

    y_gla = _gla(qk, gv, gr, g, p["gla_out_gain"].reshape(1, GLA_DV), batch, seq, min(GLA_SEQ_TILE, seq))

    one_hot = (np.arange(REL_BUCKETS)[:, None] == _t5_bucket_table().reshape(1, -1)).astype(np.float32)
    bias = jnp.dot(p["rel_bias"].astype(F32).T, one_hot, precision=lax.Precision.HIGHEST)
    bias = bias.reshape(SWA_Q_HEADS, SWA_WINDOW, 2 * SWA_WINDOW)
    window = _window_mask()
    no_prev = window & (np.arange(2 * SWA_WINDOW)[None, :] >= SWA_WINDOW)
    bias = jnp.where(np.stack([window, no_prev])[:, None], bias[None], -jnp.inf)
    bias = bias.reshape(2, SWA_Q_HEADS // 2, 2, SWA_WINDOW, 2 * SWA_WINDOW)
    bias = jnp.transpose(bias, (0, 1, 3, 2, 4)).reshape(2, SWA_KV_HEADS, -1, 4 * SWA_WINDOW)
    q_scale = jnp.tile(p["swa_q_gain"], SWA_Q_HEADS).reshape(1, nq) * (SWA_HEAD_DIM ** -0.5)
    k_scale = jnp.tile(p["swa_k_gain"], 2 * SWA_KV_HEADS).reshape(1, 2 * nkv)
    y_swa = _swa(sq, skd, svd, bias, q_scale, k_scale,
                 jnp.broadcast_to(p["swa_sinks"].reshape(SWA_Q_HEADS, 1, 1), (SWA_Q_HEADS, 1, LANES)),
                 batch, seq)

    sk = p["peer_sub_keys"].reshape(2 * PEER_HEADS, PEER_N_KEYS, PEER_HALF_DIM)
    h_mid, hnt, rank2, e2, count, coef = _merge_select(
        h2, y_gla, y_swa, gates, p["w_branch_gla"].astype(BF16),
        p["w_branch_swa"].astype(BF16), p["w_out"].astype(BF16),
        p["norm_ffn_gain"].reshape(1, d), p["peer_w_query"].astype(BF16), sk, min(SELECT_TILE, t))
    dn = _pack_rows(p["peer_expert_down"], PACK_TILE)
    upt = _pack_transposed(p["peer_expert_up"], PACK_TILE)
    return _peer_dense(hnt, h_mid, dn, upt, rank2, e2, count, coef, tm, EXPERT_BLOCK, EXPERT_SPLIT)


def kernel(x, norm_mix_gain, w_in, w_gate_up, b_gate_up, gla_out_gain, swa_q_gain,
           swa_k_gain, swa_sinks, rel_bias, w_branch_gla, w_branch_swa, w_out,
           norm_ffn_gain, peer_w_query, peer_sub_keys, peer_expert_down, peer_expert_up):
    batch, seq, d = x.shape
    h2 = x.reshape(batch * seq, d)
    for layer in range(w_in.shape[0]):
        p = dict(norm_mix_gain=norm_mix_gain[layer], w_in=w_in[layer],
                 w_gate_up=w_gate_up[layer], b_gate_up=b_gate_up[layer],
                 gla_out_gain=gla_out_gain[layer], swa_q_gain=swa_q_gain[layer],
                 swa_k_gain=swa_k_gain[layer], swa_sinks=swa_sinks[layer],
                 rel_bias=rel_bias, w_branch_gla=w_branch_gla[layer],
                 w_branch_swa=w_branch_swa[layer], w_out=w_out[layer],
                 norm_ffn_gain=norm_ffn_gain[layer], peer_w_query=peer_w_query[layer],
                 peer_sub_keys=peer_sub_keys[layer], peer_expert_down=peer_expert_down[layer],
                 peer_expert_up=peer_expert_up[layer])
        h2 = _layer(h2, batch, seq, p)
    return h2.reshape(batch, seq, d)
```

```python
import functools
import math

import numpy as np
import jax
import jax.numpy as jnp
from jax import lax
from jax.experimental import pallas as pl
from jax.experimental.pallas import tpu as pltpu

F32 = jnp.float32
BF16 = jnp.bfloat16

EPS = 1e-6

GLA_HEADS = 4
GLA_DK = 128
GLA_DV = 256
GLA_GATE_RANK = 16
GLA_GATE_NORMALIZER = 16.0
GLA_CHUNK = 64
GLA_SUB = 16

SWA_Q_HEADS = 16
SWA_KV_HEADS = 4
SWA_HEAD_DIM = 64
SWA_WINDOW = 128
REL_BUCKETS = 32
REL_MAX_DISTANCE = 128

PEER_HEADS = 8
PEER_N_KEYS = 128
PEER_HALF_DIM = 128
PEER_TOPK = 16

LANES = 128
SUBLANES = 8
BF16_ROWS = 16
VMEM_LIMIT = 56 * 1024 * 1024

TOKEN_TILE = 512
SELECT_TILE = 256
GLA_SEQ_TILE = 512
EXPERT_BLOCK = 2048
EXPERT_SPLIT = 2
PACK_TILE = 1024

NT_DIMS = (((1,), (1,)), ((), ()))
TN_DIMS = (((0,), (0,)), ((), ()))


def _cparams(semantics):
    return pltpu.CompilerParams(dimension_semantics=semantics,
                                vmem_limit_bytes=VMEM_LIMIT)


def _const_spec(shape):
    nd = len(shape)
    return pl.BlockSpec(shape, lambda *_: (0,) * nd)


def _split_bf16(x):
    hi = x.astype(BF16)
    lo = (x - hi.astype(F32)).astype(BF16)
    return hi, lo


def _pack_rows_kernel(x_ref, o_ref):
    o_ref[...] = pltpu.bitcast(x_ref[...].astype(BF16), jnp.uint32)


def _pack_transposed_kernel(x_ref, o_ref):
    o_ref[...] = pltpu.bitcast(x_ref[...].T.astype(BF16), jnp.uint32)


def _pack_transposed(x, tr):
    r, c = x.shape
    return pl.pallas_call(
        _pack_transposed_kernel,
        grid=(r // tr,),
        in_specs=[pl.BlockSpec((tr, c), lambda i: (i, 0))],
        out_specs=pl.BlockSpec((c // 2, tr), lambda i: (0, i)),
        out_shape=jax.ShapeDtypeStruct((c // 2, r), jnp.uint32),
        compiler_params=_cparams(("parallel",)),
        name="pack_transposed",
    )(x)


def _pack_rows(x, tr):
    r, c = x.shape
    return pl.pallas_call(
        _pack_rows_kernel,
        grid=(r // tr,),
        in_specs=[pl.BlockSpec((tr, c), lambda i: (i, 0))],
        out_specs=pl.BlockSpec((tr // 2, c), lambda i: (i, 0)),
        out_shape=jax.ShapeDtypeStruct((r // 2, c), jnp.uint32),
        compiler_params=_cparams(("parallel",)),
        name="pack_rows",
    )(x)


def _in_proj_kernel(x_ref, g_ref, w_ref, wga_ref, wup_ref, bup_ref, *out_refs):
    x = x_ref[...]
    ms = jnp.mean(x * x, axis=-1, keepdims=True)
    xn = (x * lax.rsqrt(ms + EPS) * g_ref[...]).astype(BF16)
    off = 0
    for o_ref in out_refs[:-1]:
        width = o_ref.shape[1]
        o_ref[...] = jnp.dot(xn, w_ref[:, off:off + width],
                             preferred_element_type=F32).astype(o_ref.dtype)
        off += width
    a_low = jnp.dot(xn, wga_ref[...], preferred_element_type=F32)
    pre = jnp.dot(a_low, wup_ref[...], preferred_element_type=F32,
                  precision=lax.Precision.HIGHEST) + bup_ref[...]
    out_refs[-1][...] = jax.nn.log_sigmoid(pre) * (1.0 / GLA_GATE_NORMALIZER)


def _in_proj(x2, gain, w_cat, w_ga, w_up, b_up, widths, tm):
    t, d = x2.shape
    nw = w_cat.shape[1]
    ng = w_up.shape[1]
    out_shape = [jax.ShapeDtypeStruct((t, w), BF16) for w in widths]
    out_shape.append(jax.ShapeDtypeStruct((t, ng), F32))
    out_specs = [pl.BlockSpec((tm, w), lambda i: (i, 0)) for w in widths + (ng,)]
    return pl.pallas_call(
        _in_proj_kernel,
        grid=(t // tm,),
        in_specs=[pl.BlockSpec((tm, d), lambda i: (i, 0)),
                  _const_spec((1, d)),
                  _const_spec((d, nw)),
                  _const_spec(w_ga.shape),
                  _const_spec(w_up.shape),
                  _const_spec((1, ng))],
        out_specs=out_specs,
        out_shape=out_shape,
        compiler_params=_cparams(("parallel",)),
        name="in_proj",
    )(x2, gain, w_cat, w_ga, w_up, b_up)


def _gla_chunk_head(q, k, v, g, st, tril):
    c_rows, sub = GLA_CHUNK, GLA_SUB
    row_s = lax.broadcasted_iota(jnp.int32, (c_rows, sub), 0)
    sub_j = lax.broadcasted_iota(jnp.int32, (sub, sub), 1)
    sub_r = lax.broadcasted_iota(jnp.int32, (sub, 1), 0)
    both = jnp.dot(tril, jnp.concatenate(_split_bf16(g), axis=1), preferred_element_type=F32)
    b = both[:, :GLA_DK] + both[:, GLA_DK:]
    o = lax.dot_general((q * jnp.exp(b)).astype(BF16), st.astype(BF16), NT_DIMS,
                        preferred_element_type=F32)
    for j in range(c_rows // sub):
        lo, hi = j * sub, (j + 1) * sub
        b_j, k_j, q_j = b[lo:hi, :], k[lo:hi, :], q[lo:hi, :]
        b_end = b[hi - 1:hi, :]
        diag = jnp.zeros((sub, sub), F32)
        for jj in range(sub):
            dec = jnp.exp(jnp.minimum(b_j - b_j[jj:jj + 1, :], 0.0))
            colv = jnp.sum(q_j * k_j[jj:jj + 1, :] * dec, axis=-1, keepdims=True)
            colv = jnp.where(sub_r >= jj, colv, 0.0)
            diag = jnp.where(sub_j == jj, colv, diag)
        pieces = []
        if lo > 0:
            pieces.append(jnp.zeros((lo, sub), F32))
        pieces.append(diag)
        if hi < c_rows:
            pieces.append(jnp.zeros((c_rows - hi, sub), F32))
        s_col = jnp.concatenate(pieces, axis=0)
        if hi < c_rows:
            k_t = (k_j * jnp.exp(b_end - b_j)).astype(BF16)
            q_t = (q * jnp.exp(jnp.minimum(b - b_end, 0.0))).astype(BF16)
            s_off = lax.dot_general(q_t, k_t, NT_DIMS, preferred_element_type=F32)
            s_col = s_col + jnp.where(row_s >= hi, s_off, 0.0)
        o = o + jnp.dot(s_col.astype(BF16), v[lo:hi, :], preferred_element_type=F32)
    b_last = b[c_rows - 1:c_rows, :]
    k_s = (k * jnp.exp(b_last - b)).astype(BF16)
    upd = lax.dot_general(v, k_s, TN_DIMS, preferred_element_type=F32)
    return o, st * jnp.exp(b_last) + upd


def _gla_kernel(qk_ref, v_ref, r_ref, g_ref, gain_ref, o_ref, state_ref, *, n_chunks):
    c_rows = GLA_CHUNK
    nh, dk, dv = GLA_HEADS, GLA_DK, GLA_DV

    @pl.when(pl.program_id(1) == 0)
    def _():
        state_ref[...] = jnp.zeros_like(state_ref)

    row = lax.broadcasted_iota(jnp.int32, (c_rows, c_rows), 0)
    col = lax.broadcasted_iota(jnp.int32, (c_rows, c_rows), 1)
    tril = (row >= col).astype(BF16)

    def chunk(c, carry):
        rows = pl.ds(pl.multiple_of(c * c_rows, c_rows), c_rows)
        for h in range(nh):
            q = qk_ref[rows, h * dk:(h + 1) * dk].astype(F32) * (dk ** -0.5)
            k = qk_ref[rows, (nh + h) * dk:(nh + h + 1) * dk].astype(F32)
            v = v_ref[rows, h * dv:(h + 1) * dv]
            g = g_ref[rows, h * dk:(h + 1) * dk]
            o, st = _gla_chunk_head(q, k, v, g, state_ref[h], tril)
            state_ref[h] = st
            ms = jnp.mean(o * o, axis=-1, keepdims=True)
            on = o * lax.rsqrt(ms + EPS) * gain_ref[...]
            r = r_ref[rows, h * dv:(h + 1) * dv].astype(F32)
            o_ref[rows, h * dv:(h + 1) * dv] = (on * (r * jax.nn.sigmoid(r))).astype(o_ref.dtype)
        return carry

    lax.fori_loop(0, n_chunks, chunk, 0)


def _gla(qk, gv, gr, g, out_gain, batch, seq, ts):
    nh, dk, dv = GLA_HEADS, GLA_DK, GLA_DV
    tok = lambda w: pl.BlockSpec((None, ts, w), lambda b, s: (b, s, 0))
    kern = functools.partial(_gla_kernel, n_chunks=ts // GLA_CHUNK)
    out = pl.pallas_call(
        kern,
        grid=(batch, seq // ts),
        in_specs=[tok(2 * nh * dk), tok(nh * dv), tok(nh * dv), tok(nh * dk),
                  _const_spec((1, dv))],
        out_specs=tok(nh * dv),
        out_shape=jax.ShapeDtypeStruct((batch, seq, nh * dv), BF16),
        scratch_shapes=[pltpu.VMEM((nh, dv, dk), F32)],
        compiler_params=_cparams(("parallel", "arbitrary")),
        name="gla",
    )(qk.reshape(batch, seq, -1), gv.reshape(batch, seq, -1), gr.reshape(batch, seq, -1),
      g.reshape(batch, seq, -1), out_gain)
    return out.reshape(batch * seq, nh * dv)


def _t5_bucket_table():
    w = SWA_WINDOW
    qi = np.arange(w)[:, None]
    kj = np.arange(2 * w)[None, :]
    rel = np.clip(qi + w - kj, 0, None)
    max_exact = REL_BUCKETS // 2
    rel_f = np.maximum(rel, 1).astype(np.float32)
    large = max_exact + (np.log(rel_f / np.float32(max_exact))
                         / np.float32(math.log(REL_MAX_DISTANCE / max_exact))
                         * np.float32(REL_BUCKETS - max_exact)).astype(np.int32)
    large = np.minimum(large, REL_BUCKETS - 1)
    return np.where(rel < max_exact, rel, large).astype(np.int32)


def _window_mask():
    w = SWA_WINDOW
    rel = np.arange(w)[:, None] + w - np.arange(2 * w)[None, :]
    return (rel >= 0) & (rel < w)


def _segment_matrix(width):
    seg = np.arange(width)[:, None] // SWA_HEAD_DIM
    return (seg == np.arange(LANES)[None, :]).astype(np.float32)


def _dot_split(x, w_ref):
    rows = x.shape[0]
    both = jnp.dot(jnp.concatenate(_split_bf16(x), axis=0), w_ref[...], preferred_element_type=F32)
    return both[:rows] + both[rows:]


def _segment_sumsq(x, e_ref):
    return _dot_split(x * x, e_ref)


def _segment_rsqrt(ss, et_ref):
    return _dot_split(lax.rsqrt(ss * (1.0 / SWA_HEAD_DIM) + EPS), et_ref)


def _swa_kernel(q_ref, kp_ref, kc_ref, vp_ref, vc_ref, bias_ref, qs_ref, ks_ref, sink_ref,
                eq_ref, eqt_ref, ek_ref, ekt_ref, o_ref):
    w, hd = SWA_WINDOW, SWA_HEAD_DIM
    group = SWA_Q_HEADS // SWA_KV_HEADS
    n = pl.program_id(1)
    band = 2 * w
    first = (n == 0).astype(jnp.int32)
    low_k = lax.broadcasted_iota(jnp.int32, (band, LANES), 1) < hd
    low_q = lax.broadcasted_iota(jnp.int32, (w, LANES), 1) < hd

    q = q_ref[...].astype(F32)
    kd = jnp.concatenate([kp_ref[...], kc_ref[...]], axis=0).astype(F32)
    ss_q = _segment_sumsq(q, eq_ref)
    ss_k = _segment_sumsq(kd, ek_ref)
    qn = (q * _segment_rsqrt(ss_q, eqt_ref) * qs_ref[...]).astype(BF16)
    kn = (kd * _segment_rsqrt(ss_k, ekt_ref) * ks_ref[...]).astype(BF16)
    vd = jnp.concatenate([vp_ref[...], vc_ref[...]], axis=0)
    zero = jnp.zeros((band, LANES), BF16)
    ones_low = jnp.where(low_k, 1.0, 0.0).astype(BF16)
    ones_high = jnp.where(low_k, 0.0, 1.0).astype(BF16)
    k_both, v_both = [], []
    for g in range(SWA_KV_HEADS):
        kg = kn[:, g * LANES:(g + 1) * LANES]
        vg = vd[:, g * LANES:(g + 1) * LANES]
        k_both.append(jnp.concatenate([jnp.where(low_k, kg, zero), jnp.where(low_k, zero, kg)], axis=0))
        v_both.append(jnp.concatenate(
            [jnp.concatenate([jnp.where(low_k, vg, zero), ones_low], axis=1),
             jnp.concatenate([jnp.where(low_k, zero, vg), ones_high], axis=1)],
            axis=0))

    pairs_per_group = group // 2

    def logits_of(g):
        qs = jnp.concatenate([qn[:, p * LANES:(p + 1) * LANES]
                              for p in range(g * pairs_per_group, (g + 1) * pairs_per_group)], axis=0)
        lg = lax.dot_general(qs, k_both[g], NT_DIMS, preferred_element_type=F32)
        return lg + bias_ref[first, g]

    def probs_of(g, lg):
        es, sink_es = [], []
        for p in range(pairs_per_group):
            rows = slice(p * w, (p + 1) * w)
            e_p, s_p = [], []
            for half in range(2):
                hq = 2 * (g * pairs_per_group + p) + half
                sink = sink_ref[hq]
                lh = lg[rows, half * band:(half + 1) * band]
                m = jnp.maximum(jnp.broadcast_to(jnp.max(lh, axis=-1, keepdims=True), (w, LANES)),
                                sink)
                e_p.append(jnp.exp(lh - jnp.concatenate([m] * (band // LANES), axis=1)).astype(BF16))
                s_p.append(jnp.exp(sink - m))
            es.append(jnp.concatenate(e_p, axis=1))
            sink_es.append(jnp.where(low_q, s_p[0], s_p[1]))
        return jnp.concatenate(es, axis=0), jnp.concatenate(sink_es, axis=0)

    def out_of(g, e, sink_e):
        pv = jnp.dot(e, v_both[g], preferred_element_type=F32)
        return pv[:, :LANES] / (pv[:, LANES:] + sink_e)

    lead = 2
    n_groups = SWA_KV_HEADS
    logits, probs = {}, {}
    for step in range(n_groups + lead):
        if step < n_groups:
            logits[step] = logits_of(step)
        if 1 <= step <= n_groups:
            probs[step - 1] = probs_of(step - 1, logits.pop(step - 1))
        if step >= lead:
            g = step - lead
            res = out_of(g, *probs.pop(g)).astype(o_ref.dtype)
            for p in range(pairs_per_group):
                grp = g * pairs_per_group + p
                o_ref[:, grp * LANES:(grp + 1) * LANES] = res[p * w:(p + 1) * w, :]


def _swa(sq, skd, svd, bias, q_scale, k_scale, sinks, batch, seq):
    w, hd = SWA_WINDOW, SWA_HEAD_DIM
    hq, hkv = SWA_Q_HEADS, SWA_KV_HEADS
    nq, nkd = hq * hd, hkv * LANES
    eq = _segment_matrix(nq)
    ek = _segment_matrix(nkd)
    prev = lambda b, n: (b, jnp.maximum(n - 1, 0), 0)
    cur = lambda b, n: (b, n, 0)
    out = pl.pallas_call(
        _swa_kernel,
        grid=(batch, seq // w),
        in_specs=[
            pl.BlockSpec((None, w, nq), cur),
            pl.BlockSpec((None, w, nkd), prev),
            pl.BlockSpec((None, w, nkd), cur),
            pl.BlockSpec((None, w, nkd), prev),
            pl.BlockSpec((None, w, nkd), cur),
            _const_spec((2, hkv, (hq // hkv // 2) * w, 4 * w)),
            _const_spec((1, nq)),
            _const_spec((1, nkd)),
            _const_spec((hq, 1, LANES)),
            _const_spec(eq.shape), _const_spec(eq.T.shape),
            _const_spec(ek.shape), _const_spec(ek.T.shape),
        ],
        out_specs=pl.BlockSpec((None, w, nq), cur),
        out_shape=jax.ShapeDtypeStruct((batch, seq, nq), BF16),
        compiler_params=_cparams(("parallel", "parallel")),
        name="swa",
    )(sq.reshape(batch, seq, nq), *([skd.reshape(batch, seq, nkd)] * 2),
      *([svd.reshape(batch, seq, nkd)] * 2), bias, q_scale, k_scale, sinks,
      jnp.asarray(eq, BF16), jnp.asarray(eq.T, BF16), jnp.asarray(ek, BF16), jnp.asarray(ek.T, BF16))
    return out.reshape(batch * seq, nq)


def _top_values(s, k, with_rank=False):
    t = s.shape[1]
    row = lax.broadcasted_iota(jnp.int32, (k, t), 0)
    vals = jnp.zeros((k, t), F32)
    rank = jnp.full(s.shape, float(k), F32) if with_rank else None
    cur = s
    for r in range(k):
        m = jnp.max(cur, axis=0, keepdims=True)
        vals = jnp.where(row == r, m, vals)
        hit = cur == m
        if with_rank:
            rank = jnp.where(hit, float(r), rank)
        if r + 1 < k:
            cur = jnp.where(hit, -jnp.inf, cur)
    return vals, rank


def _candidate_sums(t1, t2, k):
    assert k == 16
    sub = SUBLANES
    row = lax.broadcasted_iota(jnp.int32, (sub, t1.shape[1]), 0)
    pieces = [t1[0:1, :] + t2]
    for a in range(1, sub):
        pieces.append(jnp.where(row < k // (a + 1), t1[a:a + 1, :] + t2[0:sub, :], -jnp.inf))
    pieces.append(t1[sub:k, :] + t2[0:1, :])
    return jnp.concatenate(pieces, axis=0)


def _select_head(s1, s2):
    kk = PEER_TOPK
    t1, _ = _top_values(s1, kk)
    t2, rank2 = _top_values(s2, kk, with_rank=True)
    cand = _candidate_sums(t1, t2, kk)
    tau = _top_values(cand, kk)[0][kk - 1:kk, :]
    m1 = t1[0:1, :]
    m2 = t2[0:1, :]
    z = jnp.sum(jnp.where(cand >= tau, jnp.exp(cand - (m1 + m2)), 0.0), axis=0, keepdims=True)
    count = jnp.zeros(s1.shape, F32)
    for b in range(kk):
        count = jnp.where(s1 + t2[b:b + 1, :] >= tau, float(b + 1), count)
    return (pltpu.bitcast(rank2.astype(BF16), jnp.uint32),
            pltpu.bitcast(jnp.exp(s2 - m2).astype(BF16), jnp.uint32),
            count, jnp.exp(s1 - m1) / z)


def _merge_select_kernel(x_ref, ya_ref, yb_ref, gt_ref, wa_ref, wb_ref, wo_ref, gain_ref,
                         wq_ref, sk_ref, h_ref, hnt_ref, rank_ref, e2_ref, n_ref, c_ref, sc_ref):
    d = x_ref.shape[1]

    @pl.when(pl.program_id(0) == 0)
    def _():
        sc_ref[...] = jnp.zeros_like(sc_ref)

    pa = jnp.dot(ya_ref[...], wa_ref[...], preferred_element_type=F32)
    pb = jnp.dot(yb_ref[...], wb_ref[...], preferred_element_type=F32)
    gates = gt_ref[...].astype(F32)
    merged = jax.nn.sigmoid(gates[:, :d]) * pa + jax.nn.sigmoid(gates[:, d:]) * pb
    h = x_ref[...] + jnp.dot(merged.astype(BF16), wo_ref[...], preferred_element_type=F32)
    h_ref[...] = h
    ms = jnp.mean(h * h, axis=-1, keepdims=True)
    hn_f32 = h * lax.rsqrt(ms + EPS) * gain_ref[...]
    hn = hn_f32.astype(BF16)
    hnt_ref[...] = pltpu.bitcast(hn_f32.T.astype(BF16), jnp.uint32)
    q = jnp.dot(hn, wq_ref[...], preferred_element_type=F32)
    hd = PEER_HALF_DIM
    for head in range(PEER_HEADS):
        rank_ref[head], e2_ref[head], n_ref[head], c_ref[head] = _select_head(
            sc_ref[2 * head], sc_ref[2 * head + 1])
        for hp in (2 * head, 2 * head + 1):
            sc_ref[hp] = lax.dot_general(sk_ref[hp], q[:, hp * hd:(hp + 1) * hd], NT_DIMS,
                                         preferred_element_type=F32,
                                         precision=lax.Precision.HIGHEST)


def _merge_select(x2, ya, yb, gates, wa, wb, wo, gain, wq, sk, tm):
    t, d = x2.shape
    nq = wq.shape[1]
    nhp, nk, hd = sk.shape
    nh = nhp // 2
    n_tiles = t // tm
    cur = lambda i: (jnp.minimum(i, n_tiles - 1), 0)
    prev = lambda i: (0, 0, jnp.maximum(i - 1, 0))
    tok = lambda w: pl.BlockSpec((tm, w), cur)
    return pl.pallas_call(
        _merge_select_kernel,
        grid=(n_tiles + 1,),
        in_specs=[tok(d), tok(ya.shape[1]), tok(yb.shape[1]), tok(2 * d),
                  _const_spec(wa.shape), _const_spec(wb.shape), _const_spec(wo.shape),
                  _const_spec((1, d)), _const_spec((d, nq)), _const_spec((nhp, nk, hd))],
        out_specs=[tok(d), pl.BlockSpec((d // 2, tm), lambda i: (0, jnp.minimum(i, n_tiles - 1))),
                   pl.BlockSpec((nh, nk // 2, tm), prev), pl.BlockSpec((nh, nk // 2, tm), prev),
                   pl.BlockSpec((nh, nk, tm), prev), pl.BlockSpec((nh, nk, tm), prev)],
        out_shape=[jax.ShapeDtypeStruct((t, d), F32),
                   jax.ShapeDtypeStruct((d // 2, t), jnp.uint32),
                   jax.ShapeDtypeStruct((nh, nk // 2, t), jnp.uint32),
                   jax.ShapeDtypeStruct((nh, nk // 2, t), jnp.uint32),
                   jax.ShapeDtypeStruct((nh, nk, t), F32),
                   jax.ShapeDtypeStruct((nh, nk, t), F32)],
        scratch_shapes=[pltpu.VMEM((nhp, nk, tm), F32)],
        compiler_params=_cparams(("arbitrary",)),
        name="merge_select",
    )(x2, ya, yb, gates, wa, wb, wo, gain, wq, sk)


def _gated_activation(rank_ref, e2_ref, n_ref, c_ref, act_ref, w_ref, row0, n_rows):
    nk = PEER_N_KEYS
    half = nk // 2
    pack = BF16_ROWS
    tm = act_ref.shape[1]
    for tc in range(tm // LANES):
        lanes = slice(tc * LANES, (tc + 1) * LANES)
        for hf in range(2):
            words = slice(hf * half // 2, (hf + 1) * half // 2)
            gates = [None] * n_rows
            for h in range(PEER_HEADS):
                rk = pltpu.bitcast(rank_ref[h, words, lanes], BF16)
                ev = pltpu.bitcast(e2_ref[h, words, lanes], BF16)
                for ii in range(n_rows):
                    i1 = row0 + ii
                    cnt = jnp.broadcast_to(n_ref[h, i1:i1 + 1, lanes], (pack, LANES)).astype(BF16)
                    cf = jnp.broadcast_to(c_ref[h, i1:i1 + 1, lanes], (pack, LANES)).astype(BF16)
                    cnt = jnp.concatenate([cnt] * (half // pack), axis=0)
                    cf = jnp.concatenate([cf] * (half // pack), axis=0)
                    term = jnp.where(rk < cnt, ev * cf, jnp.zeros_like(ev))
                    gates[ii] = term if gates[ii] is None else gates[ii] + term
            for ii in range(n_rows):
                rows = slice(ii * nk + hf * half, ii * nk + (hf + 1) * half)
                a = act_ref[rows, lanes]
                gelu = 0.5 * a * (1.0 + lax.erf(a * (2.0 ** -0.5)))
                w_ref[rows, lanes] = gates[ii] * gelu.astype(BF16)


def _peer_dense_kernel(hnt_ref, h_ref, dn_ref, upt_ref, rank_ref, e2_ref, n_ref, c_ref,
                       o_ref, act_ref, w_ref, acc_ref):
    e = pl.program_id(1)
    n_split, ebs, _ = act_ref.shape
    rows = ebs // PEER_N_KEYS

    @pl.when(e == 0)
    def _():
        acc_ref[...] = jnp.zeros_like(acc_ref)

    hnt = pltpu.bitcast(hnt_ref[...], BF16)
    for s in range(n_split):
        dn = pltpu.bitcast(dn_ref[s * ebs // 2:(s + 1) * ebs // 2, :], BF16)
        act_ref[s] = jnp.dot(dn, hnt, preferred_element_type=F32)
    for s in range(n_split):
        _gated_activation(rank_ref, e2_ref, n_ref, c_ref, act_ref.at[s], w_ref.at[s],
                          s * rows, rows)
    w = w_ref[...].reshape(n_split * ebs, w_ref.shape[2])
    acc_ref[...] += jnp.dot(pltpu.bitcast(upt_ref[...], BF16), w,
                            preferred_element_type=F32)

    @pl.when(e == pl.num_programs(1) - 1)
    def _():
        o_ref[...] = h_ref[...] + acc_ref[...].T


def _peer_dense(hnt, h, dn, upt, rank2, e2, count, coef, tm, eb, n_split):
    t, d = h.shape
    ne = upt.shape[1]
    nk = PEER_N_KEYS
    nh = PEER_HEADS
    rows = eb // nk
    ebs = eb // n_split
    return pl.pallas_call(
        _peer_dense_kernel,
        grid=(t // tm, ne // eb),
        in_specs=[
            pl.BlockSpec((d // 2, tm), lambda i, e: (0, i)),
            pl.BlockSpec((tm, d), lambda i, e: (i, 0)),
            pl.BlockSpec((eb // 2, d), lambda i, e: (e, 0)),
            pl.BlockSpec((d // 2, eb), lambda i, e: (0, e)),
            pl.BlockSpec((nh, nk // 2, tm), lambda i, e: (0, 0, i)),
            pl.BlockSpec((nh, nk // 2, tm), lambda i, e: (0, 0, i)),
            pl.BlockSpec((nh, rows, tm), lambda i, e: (0, e, i)),
            pl.BlockSpec((nh, rows, tm), lambda i, e: (0, e, i)),
        ],
        out_specs=pl.BlockSpec((tm, d), lambda i, e: (i, 0)),
        out_shape=jax.ShapeDtypeStruct((t, d), F32),
        scratch_shapes=[pltpu.VMEM((n_split, ebs, tm), F32),
                        pltpu.VMEM((n_split, ebs, tm), BF16),
                        pltpu.VMEM((d, tm), F32)],
        compiler_params=_cparams(("parallel", "arbitrary")),
        name="peer_dense",
    )(hnt, h, dn, upt, rank2, e2, count, coef)


def _duplicate_heads(w, n_heads):
    d = w.shape[0]
    w3 = w.reshape(d, n_heads, -1)
    return jnp.concatenate([w3, w3], axis=2).reshape(d, -1)


def _layer(h2, batch, seq, p):
    t, d = h2.shape
    hk, hv = GLA_HEADS * GLA_DK, GLA_HEADS * GLA_DV
    nq, nkv = SWA_Q_HEADS * SWA_HEAD_DIM, SWA_KV_HEADS * SWA_HEAD_DIM
    rank = GLA_GATE_RANK
    tm = min(TOKEN_TILE, t)
    w_in = p["w_in"]
    bounds = np.cumsum([0, 2 * hk, hv, hv, rank, nq, nkv, nkv, 2 * d])
    assert bounds[-1] == w_in.shape[1]
    w_qk, w_v, w_r, w_ga, w_sq, w_sk, w_sv, w_gt = (
        w_in[:, a:b] for a, b in zip(bounds[:-1], bounds[1:]))
    w_cat = jnp.concatenate(
        [w_qk, w_v, w_r, w_sq, _duplicate_heads(w_sk, SWA_KV_HEADS),
         _duplicate_heads(w_sv, SWA_KV_HEADS), w_gt], axis=1).astype(BF16)
    widths = (2 * hk, hv, hv, nq, 2 * nkv, 2 * nkv, 2 * d)
    w_ga = jnp.pad(w_ga, ((0, 0), (0, LANES - rank))).astype(BF16)
    w_up = jnp.pad(p["w_gate_up"], ((0, LANES - rank), (0, 0)))
    qk, gv, gr, sq, skd, svd, gates, g = _in_proj(
        h2, p["norm_mix_gain"].reshape(1, d), w_cat, w_ga, w_up,
        p["b_gate_up"].reshape(1, hk), widths, tm)

    y_gla = _gla(qk, gv, gr, g, p["gla_out_gain"].reshape(1, GLA_DV), batch, seq, min(GLA_SEQ_TILE, seq))

    one_hot = (np.arange(REL_BUCKETS)[:, None] == _t5_bucket_table().reshape(1, -1)).astype(np.float32)
    bias = jnp.dot(p["rel_bias"].astype(F32).T, one_hot, precision=lax.Precision.HIGHEST)
    bias = bias.reshape(SWA_Q_HEADS, SWA_WINDOW, 2 * SWA_WINDOW)
    window = _window_mask()
    no_prev = window & (np.arange(2 * SWA_WINDOW)[None, :] >= SWA_WINDOW)
    bias = jnp.where(np.stack([window, no_prev])[:, None], bias[None], -jnp.inf)
    bias = bias.reshape(2, SWA_Q_HEADS // 2, 2, SWA_WINDOW, 2 * SWA_WINDOW)
    bias = jnp.transpose(bias, (0, 1, 3, 2, 4)).reshape(2, SWA_KV_HEADS, -1, 4 * SWA_WINDOW)
    q_scale = jnp.tile(p["swa_q_gain"], SWA_Q_HEADS).reshape(1, nq) * (SWA_HEAD_DIM ** -0.5)
    k_scale = jnp.tile(p["swa_k_gain"], 2 * SWA_KV_HEADS).reshape(1, 2 * nkv)
    y_swa = _swa(sq, skd, svd, bias, q_scale, k_scale,
                 jnp.broadcast_to(p["swa_sinks"].reshape(SWA_Q_HEADS, 1, 1), (SWA_Q_HEADS, 1, LANES)),
                 batch, seq)

    sk = p["peer_sub_keys"].reshape(2 * PEER_HEADS, PEER_N_KEYS, PEER_HALF_DIM)
    h_mid, hnt, rank2, e2, count, coef = _merge_select(
        h2, y_gla, y_swa, gates, p["w_branch_gla"].astype(BF16),
        p["w_branch_swa"].astype(BF16), p["w_out"].astype(BF16),
        p["norm_ffn_gain"].reshape(1, d), p["peer_w_query"].astype(BF16), sk, min(SELECT_TILE, t))
    dn = _pack_rows(p["peer_expert_down"], PACK_TILE)
    upt = _pack_transposed(p["peer_expert_up"], PACK_TILE)
    return _peer_dense(hnt, h_mid, dn, upt, rank2, e2, count, coef, tm, EXPERT_BLOCK, EXPERT_SPLIT)


def kernel(x, norm_mix_gain, w_in, w_gate_up, b_gate_up, gla_out_gain, swa_q_gain,
           swa_k_gain, swa_sinks, rel_bias, w_branch_gla, w_branch_swa, w_out,
           norm_ffn_gain, peer_w_query, peer_sub_keys, peer_expert_down, peer_expert_up):
    batch, seq, d = x.shape
    h2 = x.reshape(batch * seq, d)
    for layer in range(w_in.shape[0]):
        p = dict(norm_mix_gain=norm_mix_gain[layer], w_in=w_in[layer],
                 w_gate_up=w_gate_up[layer], b_gate_up=b_gate_up[layer],
                 gla_out_gain=gla_out_gain[layer], swa_q_gain=swa_q_gain[layer],
                 swa_k_gain=swa_k_gain[layer], swa_sinks=swa_sinks[layer],
                 rel_bias=rel_bias, w_branch_gla=w_branch_gla[layer],
                 w_branch_swa=w_branch_swa[layer], w_out=w_out[layer],
                 norm_ffn_gain=norm_ffn_gain[layer], peer_w_query=peer_w_query[layer],
                 peer_sub_keys=peer_sub_keys[layer], peer_expert_down=peer_expert_down[layer],
                 peer_expert_up=peer_expert_up[layer])
        h2 = _layer(h2, batch, seq, p)
    return h2.reshape(batch, seq, d)
```

```python
import functools
import math

import numpy as np
import jax
import jax.numpy as jnp
from jax import lax
from jax.experimental import pallas as pl
from jax.experimental.pallas import tpu as pltpu

F32 = jnp.float32
BF16 = jnp.bfloat16

EPS = 1e-6

GLA_HEADS = 4
GLA_DK = 128
GLA_DV = 256
GLA_GATE_RANK = 16
GLA_GATE_NORMALIZER = 16.0
GLA_CHUNK = 64
GLA_SUB = 16

SWA_Q_HEADS = 16
SWA_KV_HEADS = 4
SWA_HEAD_DIM = 64
SWA_WINDOW = 128
REL_BUCKETS = 32
REL_MAX_DISTANCE = 128

PEER_HEADS = 8
PEER_N_KEYS = 128
PEER_HALF_DIM = 128
PEER_TOPK = 16

LANES = 128
SUBLANES = 8
BF16_ROWS = 16
VMEM_LIMIT = 56 * 1024 * 1024

TOKEN_TILE = 512
SELECT_TILE = 256
GLA_SEQ_TILE = 512
EXPERT_BLOCK = 2048
EXPERT_SPLIT = 2
PACK_TILE = 1024

NT_DIMS = (((1,), (1,)), ((), ()))
TN_DIMS = (((0,), (0,)), ((), ()))


def _cparams(semantics):
    return pltpu.CompilerParams(dimension_semantics=semantics,
                                vmem_limit_bytes=VMEM_LIMIT)


def _const_spec(shape):
    nd = len(shape)
    return pl.BlockSpec(shape, lambda *_: (0,) * nd)


def _split_bf16(x):
    hi = x.astype(BF16)
    lo = (x - hi.astype(F32)).astype(BF16)
    return hi, lo


def _pack_rows_kernel(x_ref, o_ref):
    o_ref[...] = pltpu.bitcast(x_ref[...].astype(BF16), jnp.uint32)


def _pack_transposed_kernel(x_ref, o_ref):
    o_ref[...] = pltpu.bitcast(x_ref[...].T.astype(BF16), jnp.uint32)


def _pack_transposed(x, tr):
    r, c = x.shape
    return pl.pallas_call(
        _pack_transposed_kernel,
        grid=(r // tr,),
        in_specs=[pl.BlockSpec((tr, c), lambda i: (i, 0))],
        out_specs=pl.BlockSpec((c // 2, tr), lambda i: (0, i)),
        out_shape=jax.ShapeDtypeStruct((c // 2, r), jnp.uint32),
        compiler_params=_cparams(("parallel",)),
        name="pack_transposed",
    )(x)


def _pack_rows(x, tr):
    r, c = x.shape
    return pl.pallas_call(
        _pack_rows_kernel,
        grid=(r // tr,),
        in_specs=[pl.BlockSpec((tr, c), lambda i: (i, 0))],
        out_specs=pl.BlockSpec((tr // 2, c), lambda i: (i, 0)),
        out_shape=jax.ShapeDtypeStruct((r // 2, c), jnp.uint32),
        compiler_params=_cparams(("parallel",)),
        name="pack_rows",
    )(x)


def _in_proj_kernel(x_ref, g_ref, w_ref, wga_ref, wup_ref, bup_ref, *out_refs):
    x = x_ref[...]
    ms = jnp.mean(x * x, axis=-1, keepdims=True)
    xn = (x * lax.rsqrt(ms + EPS) * g_ref[...]).astype(BF16)
    off = 0
    for o_ref in out_refs[:-1]:
        width = o_ref.shape[1]
        o_ref[...] = jnp.dot(xn, w_ref[:, off:off + width],
                             preferred_element_type=F32).astype(o_ref.dtype)
        off += width
    a_low = jnp.dot(xn, wga_ref[...], preferred_element_type=F32)
    pre = jnp.dot(a_low, wup_ref[...], preferred_element_type=F32,
                  precision=lax.Precision.HIGHEST) + bup_ref[...]
    out_refs[-1][...] = jax.nn.log_sigmoid(pre) * (1.0 / GLA_GATE_NORMALIZER)


def _in_proj(x2, gain, w_cat, w_ga, w_up, b_up, widths, tm):
    t, d = x2.shape
    nw = w_cat.shape[1]
    ng = w_up.shape[1]
    out_shape = [jax.ShapeDtypeStruct((t, w), BF16) for w in widths]
    out_shape.append(jax.ShapeDtypeStruct((t, ng), F32))
    out_specs = [pl.BlockSpec((tm, w), lambda i: (i, 0)) for w in widths + (ng,)]
    return pl.pallas_call(
        _in_proj_kernel,
        grid=(t // tm,),
        in_specs=[pl.BlockSpec((tm, d), lambda i: (i, 0)),
                  _const_spec((1, d)),
                  _const_spec((d, nw)),
                  _const_spec(w_ga.shape),
                  _const_spec(w_up.shape),
                  _const_spec((1, ng))],
        out_specs=out_specs,
        out_shape=out_shape,
        compiler_params=_cparams(("parallel",)),
        name="in_proj",
    )(x2, gain, w_cat, w_ga, w_up, b_up)


def _gla_chunk(q, k, v, g, st, tril):
    c_rows, sub = GLA_CHUNK, GLA_SUB
    heads = range(len(q))
    row_s = lax.broadcasted_iota(jnp.int32, (c_rows, sub), 0)
    sub_j = lax.broadcasted_iota(jnp.int32, (sub, sub), 1)
    sub_r = lax.broadcasted_iota(jnp.int32, (sub, 1), 0)
    both = [jnp.dot(tril, jnp.concatenate(_split_bf16(g[h]), axis=1), preferred_element_type=F32)
            for h in heads]
    b = [x[:, :GLA_DK] + x[:, GLA_DK:] for x in both]
    o = [lax.dot_general((q[h] * jnp.exp(b[h])).astype(BF16), st[h].astype(BF16), NT_DIMS,
                         preferred_element_type=F32) for h in heads]
    for j in range(c_rows // sub):
        lo, hi = j * sub, (j + 1) * sub
        s_col = []
        for h in heads:
            b_j, k_j, q_j = b[h][lo:hi, :], k[h][lo:hi, :], q[h][lo:hi, :]
            diag = jnp.zeros((sub, sub), F32)
            for jj in range(sub):
                dec = jnp.exp(jnp.minimum(b_j - b_j[jj:jj + 1, :], 0.0))
                colv = jnp.sum(q_j * k_j[jj:jj + 1, :] * dec, axis=-1, keepdims=True)
                colv = jnp.where(sub_r >= jj, colv, 0.0)
                diag = jnp.where(sub_j == jj, colv, diag)
            pieces = []
            if lo > 0:
                pieces.append(jnp.zeros((lo, sub), F32))
            pieces.append(diag)
            if hi < c_rows:
                pieces.append(jnp.zeros((c_rows - hi, sub), F32))
            s_col.append(jnp.concatenate(pieces, axis=0))
        if hi < c_rows:
            s_off = []
            for h in heads:
                b_end = b[h][hi - 1:hi, :]
                k_t = (k[h][lo:hi, :] * jnp.exp(b_end - b[h][lo:hi, :])).astype(BF16)
                q_t = (q[h] * jnp.exp(jnp.minimum(b[h] - b_end, 0.0))).astype(BF16)
                s_off.append(lax.dot_general(q_t, k_t, NT_DIMS, preferred_element_type=F32))
            s_col = [s_col[h] + jnp.where(row_s >= hi, s_off[h], 0.0) for h in heads]
        o = [o[h] + jnp.dot(s_col[h].astype(BF16), v[h][lo:hi, :], preferred_element_type=F32)
             for h in heads]
    new_st = []
    for h in heads:
        b_last = b[h][c_rows - 1:c_rows, :]
        k_s = (k[h] * jnp.exp(b_last - b[h])).astype(BF16)
        upd = lax.dot_general(v[h], k_s, TN_DIMS, preferred_element_type=F32)
        new_st.append(st[h] * jnp.exp(b_last) + upd)
    return o, new_st


def _gla_kernel(qk_ref, v_ref, r_ref, g_ref, gain_ref, o_ref, state_ref, *, n_chunks):
    c_rows = GLA_CHUNK
    nh, dk, dv = GLA_HEADS, GLA_DK, GLA_DV

    @pl.when(pl.program_id(1) == 0)
    def _():
        state_ref[...] = jnp.zeros_like(state_ref)

    row = lax.broadcasted_iota(jnp.int32, (c_rows, c_rows), 0)
    col = lax.broadcasted_iota(jnp.int32, (c_rows, c_rows), 1)
    tril = (row >= col).astype(BF16)

    def chunk(c, carry):
        rows = pl.ds(pl.multiple_of(c * c_rows, c_rows), c_rows)
        hs = range(nh)
        q = [qk_ref[rows, h * dk:(h + 1) * dk].astype(F32) * (dk ** -0.5) for h in hs]
        k = [qk_ref[rows, (nh + h) * dk:(nh + h + 1) * dk].astype(F32) for h in hs]
        v = [v_ref[rows, h * dv:(h + 1) * dv] for h in hs]
        g = [g_ref[rows, h * dk:(h + 1) * dk] for h in hs]
        outs, states = _gla_chunk(q, k, v, g, [state_ref[h] for h in hs], tril)
        for h in hs:
            state_ref[h] = states[h]
            o = outs[h]
            ms = jnp.mean(o * o, axis=-1, keepdims=True)
            on = o * lax.rsqrt(ms + EPS) * gain_ref[...]
            r = r_ref[rows, h * dv:(h + 1) * dv].astype(F32)
            o_ref[rows, h * dv:(h + 1) * dv] = (on * (r * jax.nn.sigmoid(r))).astype(o_ref.dtype)
        return carry

    lax.fori_loop(0, n_chunks, chunk, 0)


def _gla(qk, gv, gr, g, out_gain, batch, seq, ts):
    nh, dk, dv = GLA_HEADS, GLA_DK, GLA_DV
    tok = lambda w: pl.BlockSpec((None, ts, w), lambda b, s: (b, s, 0))
    kern = functools.partial(_gla_kernel, n_chunks=ts // GLA_CHUNK)
    out = pl.pallas_call(
        kern,
        grid=(batch, seq // ts),
        in_specs=[tok(2 * nh * dk), tok(nh * dv), tok(nh * dv), tok(nh * dk),
                  _const_spec((1, dv))],
        out_specs=tok(nh * dv),
        out_shape=jax.ShapeDtypeStruct((batch, seq, nh * dv), BF16),
        scratch_shapes=[pltpu.VMEM((nh, dv, dk), F32)],
        compiler_params=_cparams(("parallel", "arbitrary")),
        name="gla",
    )(qk.reshape(batch, seq, -1), gv.reshape(batch, seq, -1), gr.reshape(batch, seq, -1),
      g.reshape(batch, seq, -1), out_gain)
    return out.reshape(batch * seq, nh * dv)


def _t5_bucket_table():
    w = SWA_WINDOW
    qi = np.arange(w)[:, None]
    kj = np.arange(2 * w)[None, :]
    rel = np.clip(qi + w - kj, 0, None)
    max_exact = REL_BUCKETS // 2
    rel_f = np.maximum(rel, 1).astype(np.float32)
    large = max_exact + (np.log(rel_f / np.float32(max_exact))
                         / np.float32(math.log(REL_MAX_DISTANCE / max_exact))
                         * np.float32(REL_BUCKETS - max_exact)).astype(np.int32)
    large = np.minimum(large, REL_BUCKETS - 1)
    return np.where(rel < max_exact, rel, large).astype(np.int32)


def _window_mask():
    w = SWA_WINDOW
    rel = np.arange(w)[:, None] + w - np.arange(2 * w)[None, :]
    return (rel >= 0) & (rel < w)


def _segment_matrix(width):
    seg = np.arange(width)[:, None] // SWA_HEAD_DIM
    return (seg == np.arange(LANES)[None, :]).astype(np.float32)


def _dot_split(x, w_ref):
    rows = x.shape[0]
    both = jnp.dot(jnp.concatenate(_split_bf16(x), axis=0), w_ref[...], preferred_element_type=F32)
    return both[:rows] + both[rows:]


def _segment_sumsq(x, e_ref):
    return _dot_split(x * x, e_ref)


def _segment_rsqrt(ss, et_ref):
    return _dot_split(lax.rsqrt(ss * (1.0 / SWA_HEAD_DIM) + EPS), et_ref)


def _swa_kernel(q_ref, kp_ref, kc_ref, vp_ref, vc_ref, bias_ref, qs_ref, ks_ref, sink_ref,
                eq_ref, eqt_ref, ek_ref, ekt_ref, o_ref):
    w, hd = SWA_WINDOW, SWA_HEAD_DIM
    group = SWA_Q_HEADS // SWA_KV_HEADS
    n = pl.program_id(1)
    band = 2 * w
    first = (n == 0).astype(jnp.int32)
    low_k = lax.broadcasted_iota(jnp.int32, (band, LANES), 1) < hd
    low_q = lax.broadcasted_iota(jnp.int32, (w, LANES), 1) < hd

    q = q_ref[...].astype(F32)
    kd = jnp.concatenate([kp_ref[...], kc_ref[...]], axis=0).astype(F32)
    ss_q = _segment_sumsq(q, eq_ref)
    ss_k = _segment_sumsq(kd, ek_ref)
    qn = (q * _segment_rsqrt(ss_q, eqt_ref) * qs_ref[...]).astype(BF16)
    kn = (kd * _segment_rsqrt(ss_k, ekt_ref) * ks_ref[...]).astype(BF16)
    vd = jnp.concatenate([vp_ref[...], vc_ref[...]], axis=0)
    zero = jnp.zeros((band, LANES), BF16)
    ones_low = jnp.where(low_k, 1.0, 0.0).astype(BF16)
    ones_high = jnp.where(low_k, 0.0, 1.0).astype(BF16)
    k_both, v_both = [], []
    for g in range(SWA_KV_HEADS):
        kg = kn[:, g * LANES:(g + 1) * LANES]
        vg = vd[:, g * LANES:(g + 1) * LANES]
        k_both.append(jnp.concatenate([jnp.where(low_k, kg, zero), jnp.where(low_k, zero, kg)], axis=0))
        v_both.append(jnp.concatenate(
            [jnp.concatenate([jnp.where(low_k, vg, zero), ones_low], axis=1),
             jnp.concatenate([jnp.where(low_k, zero, vg), ones_high], axis=1)],
            axis=0))

    pairs_per_group = group // 2

    def logits_of(g):
        qs = jnp.concatenate([qn[:, p * LANES:(p + 1) * LANES]
                              for p in range(g * pairs_per_group, (g + 1) * pairs_per_group)], axis=0)
        lg = lax.dot_general(qs, k_both[g], NT_DIMS, preferred_element_type=F32)
        return lg + bias_ref[first, g]

    def probs_of(g, lg):
        es, sink_es = [], []
        for p in range(pairs_per_group):
            rows = slice(p * w, (p + 1) * w)
            e_p, s_p = [], []
            for half in range(2):
                hq = 2 * (g * pairs_per_group + p) + half
                sink = sink_ref[hq]
                lh = lg[rows, half * band:(half + 1) * band]
                m = jnp.maximum(jnp.broadcast_to(jnp.max(lh, axis=-1, keepdims=True), (w, LANES)),
                                sink)
                e_p.append(jnp.exp(lh - jnp.concatenate([m] * (band // LANES), axis=1)).astype(BF16))
                s_p.append(jnp.exp(sink - m))
            es.append(jnp.concatenate(e_p, axis=1))
            sink_es.append(jnp.where(low_q, s_p[0], s_p[1]))
        return jnp.concatenate(es, axis=0), jnp.concatenate(sink_es, axis=0)

    def out_of(g, e, sink_e):
        pv = jnp.dot(e, v_both[g], preferred_element_type=F32)
        return pv[:, :LANES] / (pv[:, LANES:] + sink_e)

    lead = 2
    n_groups = SWA_KV_HEADS
    logits, probs = {}, {}
    for step in range(n_groups + lead):
        if step < n_groups:
            logits[step] = logits_of(step)
        if 1 <= step <= n_groups:
            probs[step - 1] = probs_of(step - 1, logits.pop(step - 1))
        if step >= lead:
            g = step - lead
            res = out_of(g, *probs.pop(g)).astype(o_ref.dtype)
            for p in range(pairs_per_group):
                grp = g * pairs_per_group + p
                o_ref[:, grp * LANES:(grp + 1) * LANES] = res[p * w:(p + 1) * w, :]


def _swa(sq, skd, svd, bias, q_scale, k_scale, sinks, batch, seq):
    w, hd = SWA_WINDOW, SWA_HEAD_DIM
    hq, hkv = SWA_Q_HEADS, SWA_KV_HEADS
    nq, nkd = hq * hd, hkv * LANES
    eq = _segment_matrix(nq)
    ek = _segment_matrix(nkd)
    prev = lambda b, n: (b, jnp.maximum(n - 1, 0), 0)
    cur = lambda b, n: (b, n, 0)
    out = pl.pallas_call(
        _swa_kernel,
        grid=(batch, seq // w),
        in_specs=[
            pl.BlockSpec((None, w, nq), cur),
            pl.BlockSpec((None, w, nkd), prev),
            pl.BlockSpec((None, w, nkd), cur),
            pl.BlockSpec((None, w, nkd), prev),
            pl.BlockSpec((None, w, nkd), cur),
            _const_spec((2, hkv, (hq // hkv // 2) * w, 4 * w)),
            _const_spec((1, nq)),
            _const_spec((1, nkd)),
            _const_spec((hq, 1, LANES)),
            _const_spec(eq.shape), _const_spec(eq.T.shape),
            _const_spec(ek.shape), _const_spec(ek.T.shape),
        ],
        out_specs=pl.BlockSpec((None, w, nq), cur),
        out_shape=jax.ShapeDtypeStruct((batch, seq, nq), BF16),
        compiler_params=_cparams(("parallel", "parallel")),
        name="swa",
    )(sq.reshape(batch, seq, nq), *([skd.reshape(batch, seq, nkd)] * 2),
      *([svd.reshape(batch, seq, nkd)] * 2), bias, q_scale, k_scale, sinks,
      jnp.asarray(eq, BF16), jnp.asarray(eq.T, BF16), jnp.asarray(ek, BF16), jnp.asarray(ek.T, BF16))
    return out.reshape(batch * seq, nq)


def _top_values(s, k, with_rank=False):
    t = s.shape[1]
    row = lax.broadcasted_iota(jnp.int32, (k, t), 0)
    vals = jnp.zeros((k, t), F32)
    rank = jnp.full(s.shape, float(k), F32) if with_rank else None
    cur = s
    for r in range(k):
        m = jnp.max(cur, axis=0, keepdims=True)
        vals = jnp.where(row == r, m, vals)
        hit = cur == m
        if with_rank:
            rank = jnp.where(hit, float(r), rank)
        if r + 1 < k:
            cur = jnp.where(hit, -jnp.inf, cur)
    return vals, rank


def _candidate_sums(t1, t2, k):
    assert k == 16
    sub = SUBLANES
    row = lax.broadcasted_iota(jnp.int32, (sub, t1.shape[1]), 0)
    pieces = [t1[0:1, :] + t2]
    for a in range(1, sub):
        pieces.append(jnp.where(row < k // (a + 1), t1[a:a + 1, :] + t2[0:sub, :], -jnp.inf))
    pieces.append(t1[sub:k, :] + t2[0:1, :])
    return jnp.concatenate(pieces, axis=0)


def _select_head(s1, s2):
    kk = PEER_TOPK
    t1, _ = _top_values(s1, kk)
    t2, rank2 = _top_values(s2, kk, with_rank=True)
    cand = _candidate_sums(t1, t2, kk)
    tau = _top_values(cand, kk)[0][kk - 1:kk, :]
    m1 = t1[0:1, :]
    m2 = t2[0:1, :]
    z = jnp.sum(jnp.where(cand >= tau, jnp.exp(cand - (m1 + m2)), 0.0), axis=0, keepdims=True)
    count = jnp.zeros(s1.shape, F32)
    for b in range(kk):
        count = jnp.where(s1 + t2[b:b + 1, :] >= tau, float(b + 1), count)
    return (pltpu.bitcast(rank2.astype(BF16), jnp.uint32),
            pltpu.bitcast(jnp.exp(s2 - m2).astype(BF16), jnp.uint32),
            count, jnp.exp(s1 - m1) / z)


def _merge_select_kernel(x_ref, ya_ref, yb_ref, gt_ref, wa_ref, wb_ref, wo_ref, gain_ref,
                         wq_ref, sk_ref, h_ref, hnt_ref, rank_ref, e2_ref, n_ref, c_ref, sc_ref):
    d = x_ref.shape[1]

    @pl.when(pl.program_id(0) == 0)
    def _():
        sc_ref[...] = jnp.zeros_like(sc_ref)

    pa = jnp.dot(ya_ref[...], wa_ref[...], preferred_element_type=F32)
    pb = jnp.dot(yb_ref[...], wb_ref[...], preferred_element_type=F32)
    gates = gt_ref[...].astype(F32)
    merged = jax.nn.sigmoid(gates[:, :d]) * pa + jax.nn.sigmoid(gates[:, d:]) * pb
    h = x_ref[...] + jnp.dot(merged.astype(BF16), wo_ref[...], preferred_element_type=F32)
    h_ref[...] = h
    ms = jnp.mean(h * h, axis=-1, keepdims=True)
    hn_f32 = h * lax.rsqrt(ms + EPS) * gain_ref[...]
    hn = hn_f32.astype(BF16)
    hnt_ref[...] = pltpu.bitcast(hn_f32.T.astype(BF16), jnp.uint32)
    q = jnp.dot(hn, wq_ref[...], preferred_element_type=F32)
    hd = PEER_HALF_DIM
    for head in range(PEER_HEADS):
        rank_ref[head], e2_ref[head], n_ref[head], c_ref[head] = _select_head(
            sc_ref[2 * head], sc_ref[2 * head + 1])
        for hp in (2 * head, 2 * head + 1):
            sc_ref[hp] = lax.dot_general(sk_ref[hp], q[:, hp * hd:(hp + 1) * hd], NT_DIMS,
                                         preferred_element_type=F32,
                                         precision=lax.Precision.HIGHEST)


def _merge_select(x2, ya, yb, gates, wa, wb, wo, gain, wq, sk, tm):
    t, d = x2.shape
    nq = wq.shape[1]
    nhp, nk, hd = sk.shape
    nh = nhp // 2
    n_tiles = t // tm
    cur = lambda i: (jnp.minimum(i, n_tiles - 1), 0)
    prev = lambda i: (0, 0, jnp.maximum(i - 1, 0))
    tok = lambda w: pl.BlockSpec((tm, w), cur)
    return pl.pallas_call(
        _merge_select_kernel,
        grid=(n_tiles + 1,),
        in_specs=[tok(d), tok(ya.shape[1]), tok(yb.shape[1]), tok(2 * d),
                  _const_spec(wa.shape), _const_spec(wb.shape), _const_spec(wo.shape),
                  _const_spec((1, d)), _const_spec((d, nq)), _const_spec((nhp, nk, hd))],
        out_specs=[tok(d), pl.BlockSpec((d // 2, tm), lambda i: (0, jnp.minimum(i, n_tiles - 1))),
                   pl.BlockSpec((nh, nk // 2, tm), prev), pl.BlockSpec((nh, nk // 2, tm), prev),
                   pl.BlockSpec((nh, nk, tm), prev), pl.BlockSpec((nh, nk, tm), prev)],
        out_shape=[jax.ShapeDtypeStruct((t, d), F32),
                   jax.ShapeDtypeStruct((d // 2, t), jnp.uint32),
                   jax.ShapeDtypeStruct((nh, nk // 2, t), jnp.uint32),
                   jax.ShapeDtypeStruct((nh, nk // 2, t), jnp.uint32),
                   jax.ShapeDtypeStruct((nh, nk, t), F32),
                   jax.ShapeDtypeStruct((nh, nk, t), F32)],
        scratch_shapes=[pltpu.VMEM((nhp, nk, tm), F32)],
        compiler_params=_cparams(("arbitrary",)),
        name="merge_select",
    )(x2, ya, yb, gates, wa, wb, wo, gain, wq, sk)


def _gated_activation(rank_ref, e2_ref, n_ref, c_ref, act_ref, w_ref, row0, n_rows):
    nk = PEER_N_KEYS
    half = nk // 2
    pack = BF16_ROWS
    tm = act_ref.shape[1]
    for tc in range(tm // LANES):
        lanes = slice(tc * LANES, (tc + 1) * LANES)
        for hf in range(2):
            words = slice(hf * half // 2, (hf + 1) * half // 2)
            gates = [None] * n_rows
            for h in range(PEER_HEADS):
                rk = pltpu.bitcast(rank_ref[h, words, lanes], BF16)
                ev = pltpu.bitcast(e2_ref[h, words, lanes], BF16)
                for ii in range(n_rows):
                    i1 = row0 + ii
                    cnt = jnp.broadcast_to(n_ref[h, i1:i1 + 1, lanes], (pack, LANES)).astype(BF16)
                    cf = jnp.broadcast_to(c_ref[h, i1:i1 + 1, lanes], (pack, LANES)).astype(BF16)
                    cnt = jnp.concatenate([cnt] * (half // pack), axis=0)
                    cf = jnp.concatenate([cf] * (half // pack), axis=0)
                    term = jnp.where(rk < cnt, ev * cf, jnp.zeros_like(ev))
                    gates[ii] = term if gates[ii] is None else gates[ii] + term
            for ii in range(n_rows):
                rows = slice(ii * nk + hf * half, ii * nk + (hf + 1) * half)
                a = act_ref[rows, lanes]
                gelu = 0.5 * a * (1.0 + lax.erf(a * (2.0 ** -0.5)))
                w_ref[rows, lanes] = gates[ii] * gelu.astype(BF16)


def _peer_dense_kernel(hnt_ref, h_ref, dn_ref, upt_ref, rank_ref, e2_ref, n_ref, c_ref,
                       o_ref, act_ref, w_ref, acc_ref):
    e = pl.program_id(1)
    n_split, ebs, _ = act_ref.shape
    rows = ebs // PEER_N_KEYS

    @pl.when(e == 0)
    def _():
        acc_ref[...] = jnp.zeros_like(acc_ref)

    hnt = pltpu.bitcast(hnt_ref[...], BF16)
    for s in range(n_split):
        dn = pltpu.bitcast(dn_ref[s * ebs // 2:(s + 1) * ebs // 2, :], BF16)
        act_ref[s] = jnp.dot(dn, hnt, preferred_element_type=F32)
    for s in range(n_split):
        _gated_activation(rank_ref, e2_ref, n_ref, c_ref, act_ref.at[s], w_ref.at[s],
                          s * rows, rows)
    w = w_ref[...].reshape(n_split * ebs, w_ref.shape[2])
    acc_ref[...] += jnp.dot(pltpu.bitcast(upt_ref[...], BF16), w,
                            preferred_element_type=F32)

    @pl.when(e == pl.num_programs(1) - 1)
    def _():
        o_ref[...] = h_ref[...] + acc_ref[...].T


def _peer_dense(hnt, h, dn, upt, rank2, e2, count, coef, tm, eb, n_split):
    t, d = h.shape
    ne = upt.shape[1]
    nk = PEER_N_KEYS
    nh = PEER_HEADS
    rows = eb // nk
    ebs = eb // n_split
    return pl.pallas_call(
        _peer_dense_kernel,
        grid=(t // tm, ne // eb),
        in_specs=[
            pl.BlockSpec((d // 2, tm), lambda i, e: (0, i)),
            pl.BlockSpec((tm, d), lambda i, e: (i, 0)),
            pl.BlockSpec((eb // 2, d), lambda i, e: (e, 0)),
            pl.BlockSpec((d // 2, eb), lambda i, e: (0, e)),
            pl.BlockSpec((nh, nk // 2, tm), lambda i, e: (0, 0, i)),
            pl.BlockSpec((nh, nk // 2, tm), lambda i, e: (0, 0, i)),
            pl.BlockSpec((nh, rows, tm), lambda i, e: (0, e, i)),
            pl.BlockSpec((nh, rows, tm), lambda i, e: (0, e, i)),
        ],
        out_specs=pl.BlockSpec((tm, d), lambda i, e: (i, 0)),
        out_shape=jax.ShapeDtypeStruct((t, d), F32),
        scratch_shapes=[pltpu.VMEM((n_split, ebs, tm), F32),
                        pltpu.VMEM((n_split, ebs, tm), BF16),
                        pltpu.VMEM((d, tm), F32)],
        compiler_params=_cparams(("parallel", "arbitrary")),
        name="peer_dense",
    )(hnt, h, dn, upt, rank2, e2, count, coef)


def _duplicate_heads(w, n_heads):
    d = w.shape[0]
    w3 = w.reshape(d, n_heads, -1)
    return jnp.concatenate([w3, w3], axis=2).reshape(d, -1)


def _layer(h2, batch, seq, p):
    t, d = h2.shape
    hk, hv = GLA_HEADS * GLA_DK, GLA_HEADS * GLA_DV
    nq, nkv = SWA_Q_HEADS * SWA_HEAD_DIM, SWA_KV_HEADS * SWA_HEAD_DIM
    rank = GLA_GATE_RANK
    tm = min(TOKEN_TILE, t)
    w_in = p["w_in"]
    bounds = np.cumsum([0, 2 * hk, hv, hv, rank, nq, nkv, nkv, 2 * d])
    assert bounds[-1] == w_in.shape[1]
    w_qk, w_v, w_r, w_ga, w_sq, w_sk, w_sv, w_gt = (
        w_in[:, a:b] for a, b in zip(bounds[:-1], bounds[1:]))
    w_cat = jnp.concatenate(
        [w_qk, w_v, w_r, w_sq, _duplicate_heads(w_sk, SWA_KV_HEADS),
         _duplicate_heads(w_sv, SWA_KV_HEADS), w_gt], axis=1).astype(BF16)
    widths = (2 * hk, hv, hv, nq, 2 * nkv, 2 * nkv, 2 * d)
    w_ga = jnp.pad(w_ga, ((0, 0), (0, LANES - rank))).astype(BF16)
    w_up = jnp.pad(p["w_gate_up"], ((0, LANES - rank), (0, 0)))
    qk, gv, gr, sq, skd, svd, gates, g = _in_proj(
        h2, p["norm_mix_gain"].reshape(1, d), w_cat, w_ga, w_up,
        p["b_gate_up"].reshape(1, hk), widths, tm)

    y_gla = _gla(qk, gv, gr, g, p["gla_out_gain"].reshape(1, GLA_DV), batch, seq, min(GLA_SEQ_TILE, seq))

    one_hot = (np.arange(REL_BUCKETS)[:, None] == _t5_bucket_table().reshape(1, -1)).astype(np.float32)
    bias = jnp.dot(p["rel_bias"].astype(F32).T, one_hot, precision=lax.Precision.HIGHEST)
    bias = bias.reshape(SWA_Q_HEADS, SWA_WINDOW, 2 * SWA_WINDOW)
    window = _window_mask()
    no_prev = window & (np.arange(2 * SWA_WINDOW)[None, :] >= SWA_WINDOW)
    bias = jnp.where(np.stack([window, no_prev])[:, None], bias[None], -jnp.inf)
    bias = bias.reshape(2, SWA_Q_HEADS // 2, 2, SWA_WINDOW, 2 * SWA_WINDOW)
    bias = jnp.transpose(bias, (0, 1, 3, 2, 4)).reshape(2, SWA_KV_HEADS, -1, 4 * SWA_WINDOW)
    q_scale = jnp.tile(p["swa_q_gain"], SWA_Q_HEADS).reshape(1, nq) * (SWA_HEAD_DIM ** -0.5)
    k_scale = jnp.tile(p["swa_k_gain"], 2 * SWA_KV_HEADS).reshape(1, 2 * nkv)
    y_swa = _swa(sq, skd, svd, bias, q_scale, k_scale,
                 jnp.broadcast_to(p["swa_sinks"].reshape(SWA_Q_HEADS, 1, 1), (SWA_Q_HEADS, 1, LANES)),
                 batch, seq)

    sk = p["peer_sub_keys"].reshape(2 * PEER_HEADS, PEER_N_KEYS, PEER_HALF_DIM)
    h_mid, hnt, rank2, e2, count, coef = _merge_select(
        h2, y_gla, y_swa, gates, p["w_branch_gla"].astype(BF16),
        p["w_branch_swa"].astype(BF16), p["w_out"].astype(BF16),
        p["norm_ffn_gain"].reshape(1, d), p["peer_w_query"].astype(BF16), sk, min(SELECT_TILE, t))
    dn = _pack_rows(p["peer_expert_down"], PACK_TILE)
    upt = _pack_transposed(p["peer_expert_up"], PACK_TILE)
    return _peer_dense(hnt, h_mid, dn, upt, rank2, e2, count, coef, tm, EXPERT_BLOCK, EXPERT_SPLIT)


def kernel(x, norm_mix_gain, w_in, w_gate_up, b_gate_up, gla_out_gain, swa_q_gain,
           swa_k_gain, swa_sinks, rel_bias, w_branch_gla, w_branch_swa, w_out,
           norm_ffn_gain, peer_w_query, peer_sub_keys, peer_expert_down, peer_expert_up):
    batch, seq, d = x.shape
    h2 = x.reshape(batch * seq, d)
    for layer in range(w_in.shape[0]):
        p = dict(norm_mix_gain=norm_mix_gain[layer], w_in=w_in[layer],
                 w_gate_up=w_gate_up[layer], b_gate_up=b_gate_up[layer],
                 gla_out_gain=gla_out_gain[layer], swa_q_gain=swa_q_gain[layer],
                 swa_k_gain=swa_k_gain[layer], swa_sinks=swa_sinks[layer],
                 rel_bias=rel_bias, w_branch_gla=w_branch_gla[layer],
                 w_branch_swa=w_branch_swa[layer], w_out=w_out[layer],
                 norm_ffn_gain=norm_ffn_gain[layer], peer_w_query=peer_w_query[layer],
                 peer_sub_keys=peer_sub_keys[layer], peer_expert_down=peer_expert_down[layer],
                 peer_expert_up=peer_expert_up[layer])
        h2 = _layer(h2, batch, seq, p)
    return h2.reshape(batch, seq, d)
```
